```python
import jax, jax.numpy as jnp
from jax import lax
import numpy as np

D_MODEL = 2048
BATCH = 4
SEQ = 4096
DEPTH = 4

D_CONV = 512
D_HGRN = 768
D_RET = 768
D_MIX = D_CONV + D_HGRN + D_RET
CONV_WIDTH = 31
HGRN_HEAD_DIM = 128
HGRN_HEADS = D_HGRN // HGRN_HEAD_DIM
RET_HEAD_DIM = 128
RET_HEADS = D_RET // RET_HEAD_DIM
CHUNK = 64
D_FF = 4 * D_MODEL
ROPE_BASE = 10000.0
LN_EPS = 1e-5
DEEPNORM_ALPHA = (2.0 * DEPTH) ** 0.25
DEEPNORM_BETA = (8.0 * DEPTH) ** -0.25
SPLITS = (D_CONV, D_CONV,
          D_HGRN, D_HGRN, D_HGRN, D_HGRN,
          D_RET, D_RET, D_RET, D_RET)
D_IN = sum(SPLITS)

kernel_name = 'hybrid_conv_hgrn2_retention_block'


def layer_norm(x, g, b):
    xf = x.astype(jnp.float32)
    mu = jnp.mean(xf, axis=-1, keepdims=True)
    var = jnp.mean(jnp.square(xf - mu), axis=-1, keepdims=True)
    return (xf - mu) * lax.rsqrt(var + LN_EPS) * g + b


def rope(x, positions):
    half = x.shape[-1] // 2
    inv = ROPE_BASE ** (-jnp.arange(half, dtype=jnp.float32) / half)
    ang = positions.astype(jnp.float32)[..., None] * inv
    cos = jnp.cos(ang)[:, :, None, :]
    sin = jnp.sin(ang)[:, :, None, :]
    x1, x2 = x[..., :half], x[..., half:]
    return jnp.concatenate([x1 * cos - x2 * sin, x1 * sin + x2 * cos], axis=-1)


def to_chunks(x):
    b, t, h, d = x.shape
    return x.reshape(b, t // CHUNK, CHUNK, h, d).transpose(1, 0, 3, 2, 4)


def from_chunks(x):
    n, b, h, c, d = x.shape
    return x.transpose(1, 0, 3, 2, 4).reshape(b, n * c, h, d)


def conv_mixer(a_in, gate_in, w_dw, b_dw, ln_g, ln_b):
    u = a_in.astype(jnp.float32) * jax.nn.sigmoid(gate_in.astype(jnp.float32))
    y = lax.conv_general_dilated(u, w_dw.astype(jnp.float32)[:, None, :], window_strides=(1,),
                                 padding=[(CONV_WIDTH - 1, 0)],
                                 dimension_numbers=('NWC', 'WIO', 'NWC'),
                                 feature_group_count=D_CONV) + b_dw
    return jax.nn.silu(layer_norm(y, ln_g, ln_b))


def hgrn2_mixer(q_in, f_in, i_in, g_in, lb, norm_g):
    b, t, _ = q_in.shape
    shp = (b, t, HGRN_HEADS, HGRN_HEAD_DIM)
    q = jax.nn.silu(q_in.astype(jnp.float32)).reshape(shp)
    inp = i_in.astype(jnp.float32).reshape(shp)
    lb = lb.astype(jnp.float32)
    log_sig = jax.nn.log_sigmoid(f_in.astype(jnp.float32))
    pos_lb = lb > 0
    log_lb = jnp.log(jnp.where(pos_lb, lb, 1.0))
    log_f = jnp.where(pos_lb, jnp.logaddexp(log_lb, jnp.log1p(-lb) + log_sig), log_sig)
    k = (-jnp.expm1(log_f)).reshape(shp)
    log_f = log_f.reshape(shp)
    causal = jnp.tril(jnp.ones((CHUNK, CHUNK), dtype=bool))[:, :, None]

    def step(S, xs):
        qc, kc, ic, lfc = xs
        cum = jnp.cumsum(lfc, axis=2)
        rel = cum[:, :, :, None, :] - cum[:, :, None, :, :]
        decay = jnp.where(causal, jnp.exp(jnp.where(causal, rel, 0.0)), 0.0)
        scores = jnp.einsum('bhtd,bhsd,bhtsd->bhts', qc, kc, decay)
        o = (jnp.einsum('bhts,bhse->bhte', scores, ic)
             + jnp.einsum('bhtd,bhde->bhte', qc * jnp.exp(cum), S))
        last = cum[:, :, -1:, :]
        S = (jnp.exp(last[:, :, 0, :])[..., None] * S
             + jnp.einsum('bhsd,bhse->bhde', kc * jnp.exp(last - cum), ic))
        return S, o

    S0 = jnp.zeros((b, HGRN_HEADS, HGRN_HEAD_DIM, HGRN_HEAD_DIM), jnp.float32)
    _, o = lax.scan(step, S0, (to_chunks(q), to_chunks(k), to_chunks(inp), to_chunks(log_f)))
    o = from_chunks(o)
    o = o * lax.rsqrt(jnp.mean(jnp.square(o), axis=-1, keepdims=True) + LN_EPS)
    return o.reshape(b, t, D_HGRN) * norm_g * jax.nn.sigmoid(g_in.astype(jnp.float32))


def retention_mixer(q_in, k_in, v_in, g_in, positions, gn_g, gn_b):
    b, t, _ = q_in.shape
    shp = (b, t, RET_HEADS, RET_HEAD_DIM)
    q = rope(q_in.astype(jnp.float32).reshape(shp), positions)
    k = rope(k_in.astype(jnp.float32).reshape(shp), positions) * RET_HEAD_DIM ** -0.5
    v = v_in.astype(jnp.float32).reshape(shp)
    log_gamma = jnp.log1p(-(2.0 ** (-5.0 - jnp.arange(RET_HEADS, dtype=jnp.float32))))
    idx = jnp.arange(CHUNK, dtype=jnp.float32)
    diff = idx[:, None] - idx[None, :]
    intra = jnp.where(diff >= 0, jnp.exp(log_gamma[:, None, None] * jnp.maximum(diff, 0.0)), 0.0)
    q_decay = jnp.exp(log_gamma[:, None] * (idx + 1.0))
    k_decay = jnp.exp(log_gamma[:, None] * (CHUNK - 1.0 - idx))
    chunk_decay = jnp.exp(log_gamma * CHUNK)

    def step(S, xs):
        qc, kc, vc = xs
        scores = jnp.einsum('bhtd,bhsd->bhts', qc, kc) * intra
        o = (jnp.einsum('bhts,bhse->bhte', scores, vc)
             + jnp.einsum('bhtd,bhde->bhte', qc * q_decay[:, :, None], S))
        S = (chunk_decay[:, None, None] * S
             + jnp.einsum('bhsd,bhse->bhde', kc * k_decay[:, :, None], vc))
        return S, o

    S0 = jnp.zeros((b, RET_HEADS, RET_HEAD_DIM, RET_HEAD_DIM), jnp.float32)
    _, o = lax.scan(step, S0, (to_chunks(q), to_chunks(k), to_chunks(v)))
    o = from_chunks(o)
    mu = jnp.mean(o, axis=-1, keepdims=True)
    var = jnp.mean(jnp.square(o - mu), axis=-1, keepdims=True)
    o = ((o - mu) * lax.rsqrt(var + LN_EPS)).reshape(b, t, D_RET) * gn_g + gn_b
    return jax.nn.silu(g_in.astype(jnp.float32)) * o


def setup_inputs(seed: int = 0) -> dict:
    key = jax.random.key(seed)
    ks = jax.random.split(key, 20)
    f32 = jnp.float32

    def nrm(k, shape, scale):
        return jax.random.normal(k, shape, f32) * scale

    x = nrm(ks[0], (BATCH, SEQ, D_MODEL), 1.0)
    positions = (jnp.arange(SEQ, dtype=jnp.int32)[None, :]
                 + jax.random.randint(ks[1], (BATCH, 1), 0, 1024, dtype=jnp.int32))
    return {
        'x': x,
        'positions': positions,
        'w_in': nrm(ks[2], (DEPTH, D_MODEL, D_IN), D_MODEL ** -0.5),
        'w_dw': nrm(ks[3], (DEPTH, CONV_WIDTH, D_CONV), CONV_WIDTH ** -0.5),
        'b_dw': nrm(ks[4], (DEPTH, D_CONV), 0.02),
        'conv_ln_g': 1.0 + nrm(ks[5], (DEPTH, D_CONV), 0.02),
        'conv_ln_b': nrm(ks[6], (DEPTH, D_CONV), 0.02),
        'hgrn_lb': nrm(ks[7], (DEPTH, D_HGRN), 0.1),
        'hgrn_norm_g': 1.0 + nrm(ks[8], (DEPTH, D_HGRN), 0.02),
        'ret_gn_g': 1.0 + nrm(ks[9], (DEPTH, D_RET), 0.02),
        'ret_gn_b': nrm(ks[10], (DEPTH, D_RET), 0.02),
        'w_out': nrm(ks[11], (DEPTH, D_MIX, D_MODEL), D_MIX ** -0.5 * DEEPNORM_BETA),
        'ln1_g': 1.0 + nrm(ks[12], (DEPTH, D_MODEL), 0.02),
        'ln1_b': nrm(ks[13], (DEPTH, D_MODEL), 0.02),
        'w_ff1': nrm(ks[14], (DEPTH, D_MODEL, D_FF), D_MODEL ** -0.5),
        'w_ff2': nrm(ks[15], (DEPTH, D_FF, D_MODEL), D_FF ** -0.5 * DEEPNORM_BETA),
        'ln2_g': 1.0 + nrm(ks[16], (DEPTH, D_MODEL), 0.02),
        'ln2_b': nrm(ks[17], (DEPTH, D_MODEL), 0.02),
    }


def reference(x, positions, w_in, w_dw, b_dw, conv_ln_g, conv_ln_b, hgrn_lb, hgrn_norm_g,
              ret_gn_g, ret_gn_b, w_out, ln1_g, ln1_b, w_ff1, w_ff2, ln2_g, ln2_b):
    dtype = x.dtype
    lbp = jax.nn.softmax(hgrn_lb.astype(jnp.float32), axis=0)
    lower_bounds = jnp.clip(jnp.cumsum(lbp, axis=0) - lbp[0], 0.0, 1.0 - 1e-6)
    split_idx = np.cumsum(SPLITS)[:-1].tolist()
    for l in range(DEPTH):
        proj = jnp.einsum('btd,de->bte', x, w_in[l])
        ca, cg, hq, hf, hi, hg, rq, rk, rv, rg = jnp.split(proj, split_idx, axis=-1)
        y_conv = conv_mixer(ca, cg, w_dw[l], b_dw[l], conv_ln_g[l], conv_ln_b[l])
        y_hgrn = hgrn2_mixer(hq, hf, hi, hg, lower_bounds[l], hgrn_norm_g[l])
        y_ret = retention_mixer(rq, rk, rv, rg, positions, ret_gn_g[l], ret_gn_b[l])
        mix = jnp.concatenate([y_conv, y_hgrn, y_ret], axis=-1).astype(dtype)
        h = jnp.einsum('bte,ed->btd', mix, w_out[l])
        x = layer_norm(DEEPNORM_ALPHA * x + h, ln1_g[l], ln1_b[l]).astype(dtype)
        f = jnp.square(jax.nn.relu(jnp.einsum('btd,df->btf', x, w_ff1[l])))
        h = jnp.einsum('btf,fd->btd', f, w_ff2[l])
        x = layer_norm(DEEPNORM_ALPHA * x + h, ln2_g[l], ln2_b[l]).astype(dtype)
    return x
```

```python
import functools
import math

import numpy as np
import jax
import jax.numpy as jnp
from jax import lax
from jax.experimental import pallas as pl
from jax.experimental.pallas import tpu as pltpu

F32 = jnp.float32
BF16 = jnp.bfloat16

D_CONV = 512
D_HGRN = 768
D_RET = 768
CONV_WIDTH = 31
HEAD_DIM = 128
HGRN_HEADS = D_HGRN // HEAD_DIM
RET_HEADS = D_RET // HEAD_DIM
ROPE_BASE = 10000.0
LN_EPS = 1e-5

HGRN_CHUNK = 64
HGRN_LEVELS = 6
RET_CHUNK = 128
CONV_HALO = 32
CONV_ROWS = 128
LANES = 128

VMEM_LIMIT_BYTES = 56 * 1024 * 1024


def _dot(a, b):
    return jnp.dot(a, b, preferred_element_type=F32)


def _dot_nt(a, b):
    return lax.dot_general(a, b, (((1,), (1,)), ((), ())), preferred_element_type=F32)


def _dot_tn(a, b):
    return lax.dot_general(a, b, (((0,), (0,)), ((), ())), preferred_element_type=F32)


def _params(*semantics):
    return pltpu.CompilerParams(dimension_semantics=semantics, vmem_limit_bytes=VMEM_LIMIT_BYTES)


def _layer_norm_rows(z, g, b):
    mu = jnp.mean(z, axis=-1, keepdims=True)
    zc = z - mu
    var = jnp.mean(zc * zc, axis=-1, keepdims=True)
    return zc * lax.rsqrt(var + LN_EPS) * g + b


def _lower_bounds_kernel(lb_ref, o_ref):
    depth = lb_ref.shape[0]
    rows = [lb_ref[l:l + 1, :] for l in range(depth)]
    m = functools.reduce(jnp.maximum, rows)
    ex = [jnp.exp(r - m) for r in rows]
    tot = functools.reduce(jnp.add, ex)
    p = [e / tot for e in ex]
    c = p[0]
    for l in range(depth):
        if l > 0:
            c = c + p[l]
        o_ref[l:l + 1, :] = jnp.clip(c - p[0], 0.0, 1.0 - 1e-6)


def _lower_bounds(hgrn_lb):
    return pl.pallas_call(
        _lower_bounds_kernel,
        name="lower_bounds",
        out_shape=jax.ShapeDtypeStruct(hgrn_lb.shape, F32),
    )(hgrn_lb.astype(F32))


def _rope_kernel(pos_ref, inv_ref, cos_ref, sin_ref):
    ang = pos_ref[...].astype(F32) * inv_ref[...]
    lane = lax.broadcasted_iota(jnp.int32, ang.shape, 1)
    s = jnp.sin(ang)
    cos_ref[...] = jnp.cos(ang)
    sin_ref[...] = jnp.where(lane < HEAD_DIM // 2, -s, s)


def _rope_tables(positions, tt):
    b, t = positions.shape
    half = HEAD_DIM // 2
    inv = ROPE_BASE ** (-jnp.arange(half, dtype=F32) / half)
    inv2 = jnp.concatenate([inv, inv]).reshape(1, HEAD_DIM)
    tab = jax.ShapeDtypeStruct((b, t, HEAD_DIM), F32)
    return pl.pallas_call(
        _rope_kernel,
        name="rope_tables",
        grid=(b, t // tt),
        in_specs=[pl.BlockSpec((None, tt, 1), lambda i, j: (i, j, 0)),
                  pl.BlockSpec((1, HEAD_DIM), lambda i, j: (0, 0))],
        out_specs=[pl.BlockSpec((None, tt, HEAD_DIM), lambda i, j: (i, j, 0))] * 2,
        out_shape=[tab, tab],
        compiler_params=_params("parallel", "parallel"),
    )(positions.reshape(b, t, 1), inv2)


def _proj_kernel(x_ref, w_ref, o_ref, xb_ref):
    @pl.when(pl.program_id(1) == 0)
    def _():
        xb_ref[...] = x_ref[...].astype(BF16)

    o_ref[...] = _dot(xb_ref[...], w_ref[...])


def _proj(x2d, w_stack, layer, col0, ncols, tm, tn):
    m, k = x2d.shape
    j0 = col0 // tn
    return pl.pallas_call(
        _proj_kernel,
        name="in_proj",
        grid=(m // tm, ncols // tn),
        in_specs=[pl.BlockSpec((tm, k), lambda i, j: (i, 0)),
                  pl.BlockSpec((None, k, tn), lambda i, j: (layer, 0, j0 + j))],
        out_specs=pl.BlockSpec((tm, tn), lambda i, j: (i, j)),
        out_shape=jax.ShapeDtypeStruct((m, ncols), F32),
        scratch_shapes=[pltpu.VMEM((tm, k), BF16)],
        compiler_params=_params("parallel", "arbitrary"),
    )(x2d, w_stack)


def _conv_kernel(a_ref, g_ref, w_ref, b_ref, lng_ref, lnb_ref, o_ref, u_ref, *, tt):
    @pl.when(pl.program_id(1) == 0)
    def _():
        u_ref[0:CONV_HALO, :] = jnp.zeros((CONV_HALO, D_CONV), F32)

    @pl.when(pl.program_id(1) > 0)
    def _():
        u_ref[0:CONV_HALO, :] = u_ref[tt:tt + CONV_HALO, :]

    u_ref[CONV_HALO:CONV_HALO + tt, :] = a_ref[...] * jax.nn.sigmoid(g_ref[...])

    first_tap = CONV_HALO - (CONV_WIDTH - 1)
    rows = min(CONV_ROWS, tt)
    for r0 in range(0, tt, rows):
        ys = []
        for c0 in range(0, D_CONV, LANES):
            acc = jnp.broadcast_to(b_ref[:, c0:c0 + LANES], (rows, LANES))
            for j in range(CONV_WIDTH):
                tap = u_ref[r0 + first_tap + j:r0 + first_tap + j + rows, c0:c0 + LANES]
                acc = acc + w_ref[j:j + 1, c0:c0 + LANES] * tap
            ys.append(acc)
        tot = functools.reduce(jnp.add, ys)
        mu = jnp.sum(tot, axis=-1, keepdims=True) * (1.0 / D_CONV)
        cs = [y - mu for y in ys]
        sq = functools.reduce(jnp.add, [c * c for c in cs])
        rstd = lax.rsqrt(jnp.sum(sq, axis=-1, keepdims=True) * (1.0 / D_CONV) + LN_EPS)
        for n, c0 in enumerate(range(0, D_CONV, LANES)):
            v = cs[n] * rstd * lng_ref[:, c0:c0 + LANES] + lnb_ref[:, c0:c0 + LANES]
            o_ref[r0:r0 + rows, c0:c0 + LANES] = (v * jax.nn.sigmoid(v)).astype(o_ref.dtype)


def _conv_mixer(pc, w_dw, b_dw, ln_g, ln_b, tt):
    b, t, _ = pc.shape
    row = lambda v: v.reshape(1, D_CONV).astype(F32)
    vec = pl.BlockSpec((1, D_CONV), lambda i, j: (0, 0))
    return pl.pallas_call(
        functools.partial(_conv_kernel, tt=tt),
        name="conv_mixer",
        grid=(b, t // tt),
        in_specs=[pl.BlockSpec((None, tt, D_CONV), lambda i, j: (i, j, 0)),
                  pl.BlockSpec((None, tt, D_CONV), lambda i, j: (i, j, 1)),
                  pl.BlockSpec((CONV_WIDTH, D_CONV), lambda i, j: (0, 0)),
                  vec, vec, vec],
        out_specs=pl.BlockSpec((None, tt, D_CONV), lambda i, j: (i, j, 0)),
        out_shape=jax.ShapeDtypeStruct((b, t, D_CONV), BF16),
        scratch_shapes=[pltpu.VMEM((CONV_HALO + tt, D_CONV), F32)],
        compiler_params=_params("parallel", "arbitrary"),
    )(pc, pc, w_dw.astype(F32), row(b_dw), row(ln_g), row(ln_b))


def _hgrn_tables():
    c = HGRN_CHUNK
    mat = np.zeros((HGRN_LEVELS + 2, c, c), np.float32)
    for l in range(HGRN_LEVELS):
        half = 1 << l
        for t in range(c):
            mid = ((t >> (l + 1)) << (l + 1)) + half - 1
            if t > mid:
                mat[l, t, mid + 1:t + 1] = 1.0
            else:
                mat[l, t, t + 1:mid + 1] = 1.0
    mat[HGRN_LEVELS] = np.tril(np.ones((c, c), np.float32))
    mat[HGRN_LEVELS + 1] = np.triu(np.ones((c, c), np.float32), 1)
    t = np.arange(c)[:, None]
    s = np.arange(c)[None, :]
    x = t ^ s
    lvl = np.zeros((c, c), np.int32)
    for l in range(1, HGRN_LEVELS):
        lvl += (x >= (1 << l)).astype(np.int32)
    lvl = np.where(t == s, HGRN_LEVELS, lvl)
    lvl = np.where(t < s, HGRN_LEVELS + 1, lvl)
    return mat.reshape(-1, c), lvl.astype(np.int32)


def _hgrn_kernel(q_ref, f_ref, i_ref, g_ref, lb_ref, ng_ref, mat_ref, lvl_ref, o_ref,
                 qs_ref, ks_ref, lf_ref, e_ref, st_ref, *, tt):
    c = HGRN_CHUNK

    @pl.when(pl.program_id(1) == 0)
    def _():
        st_ref[...] = jnp.zeros(st_ref.shape, F32)

    q = q_ref[...]
    qs_ref[...] = q * jax.nn.sigmoid(q)
    z = f_ref[...]
    ez = jnp.exp(-jnp.abs(z))
    log_sig = jnp.minimum(z, 0.0) - jnp.log1p(ez)
    lb = lb_ref[...]
    pos_lb = lb > 0
    log_lb = jnp.log(jnp.where(pos_lb, lb, 1.0))
    other = jnp.log1p(-lb) + log_sig
    lae = jnp.maximum(log_lb, other) + jnp.log1p(jnp.exp(-jnp.abs(log_lb - other)))
    log_f = jnp.where(pos_lb, lae, log_sig)
    lf_ref[...] = log_f
    ks_ref[...] = (1.0 - lb) * (jnp.where(z >= 0, ez, 1.0) / (1.0 + ez))

    lvl = lvl_ref[...]

    def chunk(ci, carry):
        r0 = pl.multiple_of(ci * c, c)
        rows = pl.ds(r0, c)
        for h in range(HGRN_HEADS):
            cols = slice(h * HEAD_DIM, (h + 1) * HEAD_DIM)
            lf = lf_ref[rows, cols]
            hi = lf.astype(BF16)
            r1 = lf - hi.astype(F32)
            mid = r1.astype(BF16)
            lo = (r1 - mid.astype(F32)).astype(BF16)
            mat = mat_ref[...]
            e_ref[...] = jnp.exp(_dot(mat, hi) + _dot(mat, mid) + _dot(mat, lo))

            qh = qs_ref[rows, cols]
            kh = ks_ref[rows, cols]
            ih = i_ref[rows, cols].astype(BF16)
            att = jnp.where(lvl == HGRN_LEVELS, _dot_nt(qh.astype(BF16), kh.astype(BF16)), 0.0)
            for l in range(HGRN_LEVELS):
                el = e_ref[l * c:(l + 1) * c, :]
                al = _dot_nt((qh * el).astype(BF16), (kh * el).astype(BF16))
                att = jnp.where(lvl == l, al, att)
            ecum = e_ref[HGRN_LEVELS * c:(HGRN_LEVELS + 1) * c, :]
            erem = e_ref[(HGRN_LEVELS + 1) * c:(HGRN_LEVELS + 2) * c, :]
            st = st_ref[h]
            o = _dot(att.astype(BF16), ih) + _dot_nt((qh * ecum).astype(BF16), st.astype(BF16))
            glast = e_ref[(HGRN_LEVELS + 1) * c - 1:(HGRN_LEVELS + 1) * c, :]
            st_ref[h] = st * glast + _dot_tn(ih, (kh * erem).astype(BF16))

            o = o * lax.rsqrt(jnp.mean(o * o, axis=-1, keepdims=True) + LN_EPS)
            o = o * ng_ref[:, cols] * jax.nn.sigmoid(g_ref[rows, cols])
            o_ref[rows, cols] = o.astype(o_ref.dtype)
        return carry

    lax.fori_loop(0, tt // c, chunk, 0)


def _hgrn_mixer(ph, lower, norm_g, tt):
    b, t, _ = ph.shape
    mat, lvl = _hgrn_tables()
    c = HGRN_CHUNK
    blk = lambda n: pl.BlockSpec((None, tt, D_HGRN), lambda i, j: (i, j, n))
    vec = pl.BlockSpec((1, D_HGRN), lambda i, j: (0, 0))
    return pl.pallas_call(
        functools.partial(_hgrn_kernel, tt=tt),
        name="hgrn_mixer",
        grid=(b, t // tt),
        in_specs=[blk(0), blk(1), blk(2), blk(3), vec, vec,
                  pl.BlockSpec(mat.shape, lambda i, j: (0, 0)),
                  pl.BlockSpec(lvl.shape, lambda i, j: (0, 0))],
        out_specs=pl.BlockSpec((None, tt, D_HGRN), lambda i, j: (i, j, 0)),
        out_shape=jax.ShapeDtypeStruct((b, t, D_HGRN), BF16),
        scratch_shapes=[pltpu.VMEM((tt, D_HGRN), F32),
                        pltpu.VMEM((tt, D_HGRN), F32),
                        pltpu.VMEM((tt, D_HGRN), F32),
                        pltpu.VMEM(((HGRN_LEVELS + 2) * c, HEAD_DIM), F32),
                        pltpu.VMEM((HGRN_HEADS, HEAD_DIM, HEAD_DIM), F32)],
        compiler_params=_params("parallel", "arbitrary"),
    )(ph, ph, ph, ph, lower.reshape(1, D_HGRN), norm_g.reshape(1, D_HGRN).astype(F32),
      jnp.asarray(mat, BF16), jnp.asarray(lvl))


def _ret_kernel(q_ref, k_ref, v_ref, g_ref, cos_ref, sin_ref, gg_ref, gb_ref, o_ref, st_ref, *, tt, c):
    @pl.when(pl.program_id(1) == 0)
    def _():
        st_ref[...] = jnp.zeros(st_ref.shape, F32)

    row = lax.broadcasted_iota(jnp.int32, (c, c), 0)
    col = lax.broadcasted_iota(jnp.int32, (c, c), 1)
    diff = (row - col).astype(F32)
    rowl = lax.broadcasted_iota(jnp.int32, (c, HEAD_DIM), 0).astype(F32)
    k_scale = HEAD_DIM ** -0.5

    for h in range(RET_HEADS):
        cols = slice(h * HEAD_DIM, (h + 1) * HEAD_DIM)
        log_gamma = math.log1p(-(2.0 ** (-5.0 - h)))
        intra = jnp.where(diff >= 0, jnp.exp(log_gamma * jnp.maximum(diff, 0.0)), 0.0)
        q_decay = jnp.exp(log_gamma * (rowl + 1.0))
        k_decay = jnp.exp(log_gamma * (c - 1.0 - rowl))
        chunk_decay = math.exp(log_gamma * c)

        def chunk(ci, carry):
            r0 = pl.multiple_of(ci * c, c)
            rows = pl.ds(r0, c)
            cs = cos_ref[rows, :]
            sn = sin_ref[rows, :]
            qh = q_ref[rows, cols]
            kh = k_ref[rows, cols]
            qr = qh * cs + pltpu.roll(qh, HEAD_DIM // 2, 1) * sn
            kr = (kh * cs + pltpu.roll(kh, HEAD_DIM // 2, 1) * sn) * k_scale
            vh = v_ref[rows, cols].astype(BF16)
            scores = _dot_nt(qr.astype(BF16), kr.astype(BF16)) * intra
            st = st_ref[h]
            o = _dot(scores.astype(BF16), vh) + _dot((qr * q_decay).astype(BF16), st.astype(BF16))
            st_ref[h] = chunk_decay * st + _dot_tn((kr * k_decay).astype(BF16), vh)

            mu = jnp.mean(o, axis=-1, keepdims=True)
            oc = o - mu
            var = jnp.mean(oc * oc, axis=-1, keepdims=True)
            on = oc * lax.rsqrt(var + LN_EPS) * gg_ref[:, cols] + gb_ref[:, cols]
            gh = g_ref[rows, cols]
            o_ref[rows, cols] = (gh * jax.nn.sigmoid(gh) * on).astype(o_ref.dtype)
            return carry

        lax.fori_loop(0, tt // c, chunk, 0)


def _ret_mixer(pr, cos_t, sin_t, gn_g, gn_b, tt):
    b, t, _ = pr.shape
    c = min(RET_CHUNK, tt)
    blk = lambda n: pl.BlockSpec((None, tt, D_RET), lambda i, j: (i, j, n))
    tab = pl.BlockSpec((None, tt, HEAD_DIM), lambda i, j: (i, j, 0))
    vec = pl.BlockSpec((1, D_RET), lambda i, j: (0, 0))
    return pl.pallas_call(
        functools.partial(_ret_kernel, tt=tt, c=c),
        name="ret_mixer",
        grid=(b, t // tt),
        in_specs=[blk(0), blk(1), blk(2), blk(3), tab, tab, vec, vec],
        out_specs=pl.BlockSpec((None, tt, D_RET), lambda i, j: (i, j, 0)),
        out_shape=jax.ShapeDtypeStruct((b, t, D_RET), BF16),
        scratch_shapes=[pltpu.VMEM((RET_HEADS, HEAD_DIM, HEAD_DIM), F32)],
        compiler_params=_params("parallel", "arbitrary"),
    )(pr, pr, pr, pr, cos_t, sin_t, gn_g.reshape(1, D_RET).astype(F32), gn_b.reshape(1, D_RET).astype(F32))


def _outproj_kernel(yc_ref, yh_ref, yr_ref, w_ref, x_ref, g_ref, b_ref, o_ref, mix_ref, *, alpha):
    mix_ref[:, 0:D_CONV] = yc_ref[...]
    mix_ref[:, D_CONV:D_CONV + D_HGRN] = yh_ref[...]
    mix_ref[:, D_CONV + D_HGRN:] = yr_ref[...]
    z = alpha * x_ref[...] + _dot(mix_ref[...], w_ref[...])
    o_ref[...] = _layer_norm_rows(z, g_ref[...], b_ref[...])


def _outproj(yc, yh, yr, w_stack, layer, x2d, ln_g, ln_b, alpha, tm):
    m, d = x2d.shape
    d_mix = D_CONV + D_HGRN + D_RET
    vec = pl.BlockSpec((1, d), lambda i: (0, 0))
    return pl.pallas_call(
        functools.partial(_outproj_kernel, alpha=alpha),
        name="out_proj_ln",
        grid=(m // tm,),
        in_specs=[pl.BlockSpec((tm, D_CONV), lambda i: (i, 0)),
                  pl.BlockSpec((tm, D_HGRN), lambda i: (i, 0)),
                  pl.BlockSpec((tm, D_RET), lambda i: (i, 0)),
                  pl.BlockSpec((None, d_mix, d), lambda i: (layer, 0, 0)),
                  pl.BlockSpec((tm, d), lambda i: (i, 0)),
                  vec, vec],
        out_specs=pl.BlockSpec((tm, d), lambda i: (i, 0)),
        out_shape=jax.ShapeDtypeStruct((m, d), F32),
        scratch_shapes=[pltpu.VMEM((tm, d_mix), BF16)],
        compiler_params=_params("parallel"),
    )(yc, yh, yr, w_stack, x2d, ln_g.reshape(1, d).astype(F32), ln_b.reshape(1, d).astype(F32))


def _ffn_kernel(x_ref, w1_ref, w2_ref, g_ref, b_ref, o_ref, xb_ref, acc_ref, *, alpha):
    f = pl.program_id(1)

    @pl.when(f == 0)
    def _():
        xb_ref[...] = x_ref[...].astype(BF16)
        acc_ref[...] = jnp.zeros(acc_ref.shape, F32)

    hid = jnp.maximum(_dot(xb_ref[...], w1_ref[...]), 0.0)
    acc_ref[...] += _dot((hid * hid).astype(BF16), w2_ref[...])

    @pl.when(f == pl.num_programs(1) - 1)
    def _():
        z = alpha * x_ref[...] + acc_ref[...]
        o_ref[...] = _layer_norm_rows(z, g_ref[...], b_ref[...])


def _ffn(x2d, w1_stack, w2_stack, layer, ln_g, ln_b, alpha, tm, tf):
    m, d = x2d.shape
    d_ff = w1_stack.shape[-1]
    vec = pl.BlockSpec((1, d), lambda i, f: (0, 0))
    return pl.pallas_call(
        functools.partial(_ffn_kernel, alpha=alpha),
        name="ffn_ln",
        grid=(m // tm, d_ff // tf),
        in_specs=[pl.BlockSpec((tm, d), lambda i, f: (i, 0)),
                  pl.BlockSpec((None, d, tf), lambda i, f: (layer, 0, f)),
                  pl.BlockSpec((None, tf, d), lambda i, f: (layer, f, 0)),
                  vec, vec],
        out_specs=pl.BlockSpec((tm, d), lambda i, f: (i, 0)),
        out_shape=jax.ShapeDtypeStruct((m, d), F32),
        scratch_shapes=[pltpu.VMEM((tm, d), BF16), pltpu.VMEM((tm, d), F32)],
        compiler_params=_params("parallel", "arbitrary"),
    )(x2d, w1_stack, w2_stack, ln_g.reshape(1, d).astype(F32), ln_b.reshape(1, d).astype(F32))


def kernel(x, positions, w_in, w_dw, b_dw, conv_ln_g, conv_ln_b, hgrn_lb, hgrn_norm_g, ret_gn_g, ret_gn_b,
           w_out, ln1_g, ln1_b, w_ff1, w_ff2, ln2_g, ln2_b):
    b, t, d = x.shape
    depth = w_in.shape[0]
    alpha = (2.0 * depth) ** 0.25
    m = b * t
    tt = min(512, t)
    tm = min(1024, m)
    tm_ffn = min(512, m)
    tn = 1024
    tf = 512

    w_in_b = w_in.astype(BF16)
    w_out_b = w_out.astype(BF16)
    w_ff1_b = w_ff1.astype(BF16)
    w_ff2_b = w_ff2.astype(BF16)

    lower = _lower_bounds(hgrn_lb)
    cos_t, sin_t = _rope_tables(positions, tt)

    conv_cols = 2 * D_CONV
    hgrn_cols = 4 * D_HGRN
    ret_cols = 4 * D_RET
    x2d = x.reshape(m, d).astype(F32)
    for l in range(depth):
        pc = _proj(x2d, w_in_b, l, 0, conv_cols, tm, tn).reshape(b, t, conv_cols)
        ph = _proj(x2d, w_in_b, l, conv_cols, hgrn_cols, tm, tn).reshape(b, t, hgrn_cols)
        pr = _proj(x2d, w_in_b, l, conv_cols + hgrn_cols, ret_cols, tm, tn).reshape(b, t, ret_cols)
        yc = _conv_mixer(pc, w_dw[l], b_dw[l], conv_ln_g[l], conv_ln_b[l], tt)
        yh = _hgrn_mixer(ph, lower[l], hgrn_norm_g[l], tt)
        yr = _ret_mixer(pr, cos_t, sin_t, ret_gn_g[l], ret_gn_b[l], tt)
        x2d = _outproj(yc.reshape(m, D_CONV), yh.reshape(m, D_HGRN), yr.reshape(m, D_RET),
                       w_out_b, l, x2d, ln1_g[l], ln1_b[l], alpha, tm_ffn)
        x2d = _ffn(x2d, w_ff1_b, w_ff2_b, l, ln2_g[l], ln2_b[l], alpha, tm_ffn, tf)
    return x2d.reshape(b, t, d).astype(x.dtype)
```

```python
import functools
import math

import numpy as np
import jax
import jax.numpy as jnp
from jax import lax
from jax.experimental import pallas as pl
from jax.experimental.pallas import tpu as pltpu

F32 = jnp.float32
BF16 = jnp.bfloat16

D_CONV = 512
D_HGRN = 768
D_RET = 768
CONV_WIDTH = 31
HEAD_DIM = 128
HGRN_HEADS = D_HGRN // HEAD_DIM
RET_HEADS = D_RET // HEAD_DIM
ROPE_BASE = 10000.0
LN_EPS = 1e-5

LANES = 128
SUBLANES = 8
HGRN_CHUNK = 64
HGRN_LEVELS = 6
HGRN_PAIRS = HGRN_HEADS // 2
HGRN_UNROLL = 2
PAIR = 2 * HEAD_DIM
RET_CHUNK = 128
CONV_HALO = 32
CONV_ROWS = 128
NORM_ROWS = 64

VMEM_LIMIT_BYTES = 56 * 1024 * 1024

HGRN_COL_BLOCK = 0
RET_COL_BLOCK = 4 * D_HGRN // D_RET
CONV_COL_BLOCK = (4 * D_HGRN + 4 * D_RET) // D_CONV


def _dot(a, b):
    return jnp.dot(a, b, preferred_element_type=F32)


def _dot_nt(a, b):
    return lax.dot_general(a, b, (((1,), (1,)), ((), ())), preferred_element_type=F32)


def _dot_tn(a, b):
    return lax.dot_general(a, b, (((0,), (0,)), ((), ())), preferred_element_type=F32)


def _params(*semantics):
    return pltpu.CompilerParams(dimension_semantics=semantics, vmem_limit_bytes=VMEM_LIMIT_BYTES)


def _layer_norm_rows(z, g, b):
    mu = jnp.mean(z, axis=-1, keepdims=True)
    zc = z - mu
    var = jnp.mean(zc * zc, axis=-1, keepdims=True)
    return zc * lax.rsqrt(var + LN_EPS) * g + b


def _lower_bounds_kernel(lb_ref, o_ref):
    depth = lb_ref.shape[0]
    rows = [lb_ref[l:l + 1, :] for l in range(depth)]
    m = functools.reduce(jnp.maximum, rows)
    ex = [jnp.exp(r - m) for r in rows]
    tot = functools.reduce(jnp.add, ex)
    p = [e / tot for e in ex]
    c = p[0]
    for l in range(depth):
        if l > 0:
            c = c + p[l]
        o_ref[l:l + 1, :] = jnp.clip(c - p[0], 0.0, 1.0 - 1e-6)


def _lower_bounds(hgrn_lb):
    return pl.pallas_call(
        _lower_bounds_kernel,
        name="lower_bounds",
        out_shape=jax.ShapeDtypeStruct(hgrn_lb.shape, F32),
    )(hgrn_lb.astype(F32))


def _rope_kernel(pos_ref, inv_ref, cos_ref, sin_ref):
    ang = pos_ref[...].astype(F32) * inv_ref[...]
    lane = lax.broadcasted_iota(jnp.int32, ang.shape, 1)
    s = jnp.sin(ang)
    cos_ref[...] = jnp.cos(ang)
    sin_ref[...] = jnp.where(lane < HEAD_DIM // 2, -s, s)


def _rope_tables(positions, tt):
    b, t = positions.shape
    half = HEAD_DIM // 2
    inv = ROPE_BASE ** (-jnp.arange(half, dtype=F32) / half)
    inv2 = jnp.concatenate([inv, inv]).reshape(1, HEAD_DIM)
    tab = jax.ShapeDtypeStruct((b, t, HEAD_DIM), F32)
    return pl.pallas_call(
        _rope_kernel,
        name="rope_tables",
        grid=(b, t // tt),
        in_specs=[pl.BlockSpec((None, tt, 1), lambda i, j: (i, j, 0)),
                  pl.BlockSpec((1, HEAD_DIM), lambda i, j: (0, 0))],
        out_specs=[pl.BlockSpec((None, tt, HEAD_DIM), lambda i, j: (i, j, 0))] * 2,
        out_shape=[tab, tab],
        compiler_params=_params("parallel", "parallel"),
    )(positions.reshape(b, t, 1), inv2)


def _proj_kernel(x_ref, w_ref, o_ref, xb_ref):
    @pl.when(pl.program_id(1) == 0)
    def _():
        xb_ref[...] = x_ref[...].astype(BF16)

    o_ref[...] = _dot(xb_ref[...], w_ref[...])


def _proj(x2d, w_stack, layer, tm, tn):
    m, k = x2d.shape
    n = w_stack.shape[-1]
    return pl.pallas_call(
        _proj_kernel,
        name="in_proj",
        grid=(m // tm, n // tn),
        in_specs=[pl.BlockSpec((tm, k), lambda i, j: (i, 0)),
                  pl.BlockSpec((None, k, tn), lambda i, j: (layer, 0, j))],
        out_specs=pl.BlockSpec((tm, tn), lambda i, j: (i, j)),
        out_shape=jax.ShapeDtypeStruct((m, n), F32),
        scratch_shapes=[pltpu.VMEM((tm, k), BF16)],
        compiler_params=_params("parallel", "arbitrary"),
    )(x2d, w_stack)


def _conv_kernel(a_ref, g_ref, w_ref, b_ref, lng_ref, lnb_ref, o_ref, u_ref, y_ref, *, tt):
    @pl.when(pl.program_id(1) == 0)
    def _():
        u_ref[0:CONV_HALO, :] = jnp.zeros((CONV_HALO, D_CONV), F32)

    @pl.when(pl.program_id(1) > 0)
    def _():
        u_ref[0:CONV_HALO, :] = u_ref[tt:tt + CONV_HALO, :]

    u_ref[CONV_HALO:CONV_HALO + tt, :] = a_ref[...] * jax.nn.sigmoid(g_ref[...])

    first_tap = CONV_HALO - (CONV_WIDTH - 1)
    rows = min(CONV_ROWS, tt)
    win = rows + CONV_HALO
    for r0 in range(0, tt, rows):
        for c0 in range(0, D_CONV, LANES):
            acc = jnp.broadcast_to(b_ref[:, c0:c0 + LANES], (rows, LANES))
            for r in range(SUBLANES):
                window = u_ref[r0:r0 + win, c0:c0 + LANES]
                shifted = window if r == 0 else pltpu.roll(window, win - r, 0)
                for a in range(win // SUBLANES):
                    j = SUBLANES * a + r - first_tap
                    if 0 <= j < CONV_WIDTH:
                        acc = acc + w_ref[j:j + 1, c0:c0 + LANES] * shifted[SUBLANES * a:SUBLANES * a + rows]
            y_ref[r0:r0 + rows, c0:c0 + LANES] = acc

    nrows = min(NORM_ROWS, tt)
    for r0 in range(0, tt, nrows):
        v = _layer_norm_rows(y_ref[r0:r0 + nrows, :], lng_ref[...], lnb_ref[...])
        o_ref[r0:r0 + nrows, :] = (v * jax.nn.sigmoid(v)).astype(o_ref.dtype)


def _conv_mixer(proj, w_dw, b_dw, ln_g, ln_b, tt):
    b, t, _ = proj.shape
    row = lambda v: v.reshape(1, D_CONV).astype(F32)
    vec = pl.BlockSpec((1, D_CONV), lambda i, j: (0, 0))
    blk = lambda n: pl.BlockSpec((None, tt, D_CONV), lambda i, j: (i, j, CONV_COL_BLOCK + n))
    return pl.pallas_call(
        functools.partial(_conv_kernel, tt=tt),
        name="conv_mixer",
        grid=(b, t // tt),
        in_specs=[blk(0), blk(1),
                  pl.BlockSpec((CONV_WIDTH, D_CONV), lambda i, j: (0, 0)),
                  vec, vec, vec],
        out_specs=pl.BlockSpec((None, tt, D_CONV), lambda i, j: (i, j, 0)),
        out_shape=jax.ShapeDtypeStruct((b, t, D_CONV), BF16),
        scratch_shapes=[pltpu.VMEM((CONV_HALO + tt, D_CONV), F32),
                        pltpu.VMEM((tt, D_CONV), F32)],
        compiler_params=_params("parallel", "arbitrary"),
    )(proj, proj, w_dw.astype(F32), row(b_dw), row(ln_g), row(ln_b))


def _hgrn_tables():
    c = HGRN_CHUNK
    mat = np.zeros((HGRN_LEVELS, c, c), np.float32)
    for l in range(1, HGRN_LEVELS):
        half = 1 << l
        for t in range(c):
            mid = ((t >> (l + 1)) << (l + 1)) + half - 1
            if t > mid:
                mat[l - 1, t, mid + 1:t + 1] = 1.0
            else:
                mat[l - 1, t, t + 1:mid + 1] = 1.0
    mat[HGRN_LEVELS - 1] = np.tril(np.ones((c, c), np.float32))
    mat = mat.reshape(-1, c)
    t = np.arange(c)[:, None]
    s = np.arange(c)[None, :]
    x = t ^ s
    lvl = np.zeros((c, c), np.int32)
    for l in range(1, HGRN_LEVELS):
        lvl += (x >= (1 << l)).astype(np.int32)
    lvl = np.where(t == s, HGRN_LEVELS, lvl)
    lvl = np.where(t < s, HGRN_LEVELS + 1, lvl)
    return np.concatenate([mat, mat, mat], axis=1), np.concatenate([lvl, lvl], axis=1).astype(np.int32)


def _hgrn_kernel(q_ref, f_ref, i_ref, g_ref, lb_ref, ng_ref, mat_ref, lvl_ref, o_ref,
                 qs_ref, ks_ref, lf_ref, e_ref, kb_ref, ib_ref, st_ref, *, tt):
    c = HGRN_CHUNK

    @pl.when(pl.program_id(1) == 0)
    def _():
        st_ref[...] = jnp.zeros(st_ref.shape, F32)
        kb_ref[...] = jnp.zeros(kb_ref.shape, BF16)
        ib_ref[...] = jnp.zeros(ib_ref.shape, BF16)

    q = q_ref[...]
    qs_ref[...] = q * jax.nn.sigmoid(q)
    z = f_ref[...]
    ez = jnp.exp(-jnp.abs(z))
    inv = 1.0 / (1.0 + ez)
    log_sig = jnp.minimum(z, 0.0) - jnp.log(1.0 + ez)
    lb = lb_ref[...]
    f_pos = lb + (1.0 - lb) * (jnp.where(z >= 0, 1.0, ez) * inv)
    lf_ref[...] = jnp.where(lb > 0, jnp.log(f_pos), log_sig)
    ks_ref[...] = (1.0 - lb) * (jnp.where(z >= 0, ez, 1.0) * inv)

    lvl = lvl_ref[...]
    odd_row = (lax.broadcasted_iota(jnp.int32, (c, PAIR), 0) & 1) == 1

    tasks = [(u, p) for u in range(HGRN_UNROLL) for p in range(HGRN_PAIRS)]

    def chunk(ci, carry):
        base = ci * (HGRN_UNROLL * c)
        rows = [pl.ds(pl.multiple_of(base + u * c, c), c) for u in range(HGRN_UNROLL)]
        lanes = [slice(p * PAIR, (p + 1) * PAIR) for p in range(HGRN_PAIRS)]
        slot = {t: n for n, t in enumerate(tasks)}

        lf, sums = {}, {}
        for t in tasks:
            u, p = t
            lf[t] = lf_ref[rows[u], lanes[p]]
            hi = lf[t].astype(BF16)
            r1 = lf[t] - hi.astype(F32)
            mid = r1.astype(BF16)
            lo = (r1 - mid.astype(F32)).astype(BF16)
            sums[t] = _dot(mat_ref[...], jnp.concatenate([hi, mid, lo], axis=0))

        q2, k2, i2, erem = {}, {}, {}, {}
        for t in tasks:
            u, p = t
            e_ref[slot[t]] = jnp.exp(sums[t])
            cum = sums[t][(HGRN_LEVELS - 1) * c:HGRN_LEVELS * c]
            erem[t] = jnp.exp(cum[c - 1:c, :] - cum)
            q2[t] = qs_ref[rows[u], lanes[p]]
            k2[t] = ks_ref[rows[u], lanes[p]]
            i2[t] = i_ref[rows[u], lanes[p]].astype(BF16)
            ib_ref[slot[t], 0:c, 0:HEAD_DIM] = i2[t][:, 0:HEAD_DIM]
            ib_ref[slot[t], c:2 * c, HEAD_DIM:PAIR] = i2[t][:, HEAD_DIM:PAIR]

        def scores(t, level, el):
            kt = (k2[t] * el).astype(BF16) if el is not None else k2[t].astype(BF16)
            qt = (q2[t] * el).astype(BF16) if el is not None else q2[t].astype(BF16)
            kb_ref[slot[t], level, 0:c, 0:HEAD_DIM] = kt[:, 0:HEAD_DIM]
            kb_ref[slot[t], level, c:2 * c, HEAD_DIM:PAIR] = kt[:, HEAD_DIM:PAIR]
            return _dot_nt(qt, kb_ref[slot[t], level])

        att = {t: jnp.where(lvl == HGRN_LEVELS, scores(t, HGRN_LEVELS, None), 0.0) for t in tasks}
        for t in tasks:
            att[t] = jnp.where(lvl == 0, scores(t, 0, jnp.exp(jnp.where(odd_row, lf[t], 0.0))), att[t])
        for l in range(1, HGRN_LEVELS):
            for t in tasks:
                att[t] = jnp.where(lvl == l, scores(t, l, e_ref[slot[t], (l - 1) * c:l * c, :]), att[t])

        o2 = {t: _dot(att[t].astype(BF16), ib_ref[slot[t]]) for t in tasks}

        for t in tasks:
            u, p = t
            ecum = e_ref[slot[t], (HGRN_LEVELS - 1) * c:HGRN_LEVELS * c, :]
            glast = e_ref[slot[t], HGRN_LEVELS * c - 1:HGRN_LEVELS * c, :]
            qc = (q2[t] * ecum).astype(BF16)
            kr = (k2[t] * erem[t]).astype(BF16)
            gate = ng_ref[:, lanes[p]] * jax.nn.sigmoid(g_ref[rows[u], lanes[p]])
            for hh in range(2):
                h = 2 * p + hh
                hl = slice(hh * HEAD_DIM, (hh + 1) * HEAD_DIM)
                st = st_ref[h]
                o = o2[t][:, hl] + _dot_nt(qc[:, hl], st.astype(BF16))
                st_ref[h] = st * glast[:, hl] + _dot_tn(i2[t][:, hl], kr[:, hl])
                o = o * lax.rsqrt(jnp.mean(o * o, axis=-1, keepdims=True) + LN_EPS)
                o_ref[rows[u], h * HEAD_DIM:(h + 1) * HEAD_DIM] = (o * gate[:, hl]).astype(o_ref.dtype)
        return carry

    lax.fori_loop(0, tt // (HGRN_UNROLL * c), chunk, 0)


def _hgrn_mixer(proj, lower, norm_g, tt):
    b, t, _ = proj.shape
    mat, lvl = _hgrn_tables()
    c = HGRN_CHUNK
    blk = lambda n: pl.BlockSpec((None, tt, D_HGRN), lambda i, j: (i, j, HGRN_COL_BLOCK + n))
    vec = pl.BlockSpec((1, D_HGRN), lambda i, j: (0, 0))
    return pl.pallas_call(
        functools.partial(_hgrn_kernel, tt=tt),
        name="hgrn_mixer",
        grid=(b, t // tt),
        in_specs=[blk(0), blk(1), blk(2), blk(3), vec, vec,
                  pl.BlockSpec(mat.shape, lambda i, j: (0, 0)),
                  pl.BlockSpec(lvl.shape, lambda i, j: (0, 0))],
        out_specs=pl.BlockSpec((None, tt, D_HGRN), lambda i, j: (i, j, 0)),
        out_shape=jax.ShapeDtypeStruct((b, t, D_HGRN), BF16),
        scratch_shapes=[pltpu.VMEM((tt, D_HGRN), F32),
                        pltpu.VMEM((tt, D_HGRN), F32),
                        pltpu.VMEM((tt, D_HGRN), F32),
                        pltpu.VMEM((HGRN_UNROLL * HGRN_PAIRS, HGRN_LEVELS * c, PAIR), F32),
                        pltpu.VMEM((HGRN_UNROLL * HGRN_PAIRS, HGRN_LEVELS + 1, 2 * c, PAIR), BF16),
                        pltpu.VMEM((HGRN_UNROLL * HGRN_PAIRS, 2 * c, PAIR), BF16),
                        pltpu.VMEM((HGRN_HEADS, HEAD_DIM, HEAD_DIM), F32)],
        compiler_params=_params("parallel", "arbitrary"),
    )(proj, proj, proj, proj, lower.reshape(1, D_HGRN), norm_g.reshape(1, D_HGRN).astype(F32),
      jnp.asarray(mat, BF16), jnp.asarray(lvl))


def _ret_kernel(q_ref, k_ref, v_ref, g_ref, cos_ref, sin_ref, gg_ref, gb_ref, o_ref,
                intra_ref, qdec_ref, kdec_ref, st_ref, *, tt, c):
    log_gamma = [math.log1p(-(2.0 ** (-5.0 - h))) for h in range(RET_HEADS)]

    @pl.when(pl.program_id(1) == 0)
    def _():
        st_ref[...] = jnp.zeros(st_ref.shape, F32)
        row = lax.broadcasted_iota(jnp.int32, (c, c), 0)
        col = lax.broadcasted_iota(jnp.int32, (c, c), 1)
        diff = (row - col).astype(F32)
        rowl = lax.broadcasted_iota(jnp.int32, (c, HEAD_DIM), 0).astype(F32)
        for h in range(RET_HEADS):
            intra_ref[h] = jnp.where(diff >= 0, jnp.exp(log_gamma[h] * jnp.maximum(diff, 0.0)), 0.0)
            qdec_ref[h] = jnp.exp(log_gamma[h] * (rowl + 1.0))
            kdec_ref[h] = jnp.exp(log_gamma[h] * (c - 1.0 - rowl))

    k_scale = HEAD_DIM ** -0.5

    def chunk(ci, carry):
        r0 = pl.multiple_of(ci * c, c)
        rows = pl.ds(r0, c)
        cs = cos_ref[rows, :]
        sn = sin_ref[rows, :]
        heads = range(RET_HEADS)
        cols = [slice(h * HEAD_DIM, (h + 1) * HEAD_DIM) for h in heads]
        qr, kr, vh, scores = [], [], [], []
        for h in heads:
            qh = q_ref[rows, cols[h]]
            kh = k_ref[rows, cols[h]]
            qr.append(qh * cs + pltpu.roll(qh, HEAD_DIM // 2, 1) * sn)
            kr.append((kh * cs + pltpu.roll(kh, HEAD_DIM // 2, 1) * sn) * k_scale)
            vh.append(v_ref[rows, cols[h]].astype(BF16))
            scores.append(_dot_nt(qr[h].astype(BF16), kr[h].astype(BF16)))
        outs = []
        for h in heads:
            st = st_ref[h]
            masked = (scores[h] * intra_ref[h]).astype(BF16)
            outs.append(_dot(masked, vh[h]) + _dot((qr[h] * qdec_ref[h]).astype(BF16), st.astype(BF16)))
            st_ref[h] = math.exp(log_gamma[h] * c) * st + _dot_tn((kr[h] * kdec_ref[h]).astype(BF16), vh[h])
        for h in heads:
            o = outs[h]
            mu = jnp.mean(o, axis=-1, keepdims=True)
            oc = o - mu
            var = jnp.mean(oc * oc, axis=-1, keepdims=True)
            on = oc * lax.rsqrt(var + LN_EPS) * gg_ref[:, cols[h]] + gb_ref[:, cols[h]]
            gh = g_ref[rows, cols[h]]
            o_ref[rows, cols[h]] = (gh * jax.nn.sigmoid(gh) * on).astype(o_ref.dtype)
        return carry

    lax.fori_loop(0, tt // c, chunk, 0)


def _ret_mixer(proj, cos_t, sin_t, gn_g, gn_b, tt):
    b, t, _ = proj.shape
    c = min(RET_CHUNK, tt)
    blk = lambda n: pl.BlockSpec((None, tt, D_RET), lambda i, j: (i, j, RET_COL_BLOCK + n))
    tab = pl.BlockSpec((None, tt, HEAD_DIM), lambda i, j: (i, j, 0))
    vec = pl.BlockSpec((1, D_RET), lambda i, j: (0, 0))
    return pl.pallas_call(
        functools.partial(_ret_kernel, tt=tt, c=c),
        name="ret_mixer",
        grid=(b, t // tt),
        in_specs=[blk(0), blk(1), blk(2), blk(3), tab, tab, vec, vec],
        out_specs=pl.BlockSpec((None, tt, D_RET), lambda i, j: (i, j, 0)),
        out_shape=jax.ShapeDtypeStruct((b, t, D_RET), BF16),
        scratch_shapes=[pltpu.VMEM((RET_HEADS, c, c), F32),
                        pltpu.VMEM((RET_HEADS, c, HEAD_DIM), F32),
                        pltpu.VMEM((RET_HEADS, c, HEAD_DIM), F32),
                        pltpu.VMEM((RET_HEADS, HEAD_DIM, HEAD_DIM), F32)],
        compiler_params=_params("parallel", "arbitrary"),
    )(proj, proj, proj, proj, cos_t, sin_t, gn_g.reshape(1, D_RET).astype(F32), gn_b.reshape(1, D_RET).astype(F32))


def _outproj_kernel(yc_ref, yh_ref, yr_ref, w_ref, x_ref, g_ref, b_ref, o_ref, mix_ref, *, alpha):
    mix_ref[:, 0:D_CONV] = yc_ref[...]
    mix_ref[:, D_CONV:D_CONV + D_HGRN] = yh_ref[...]
    mix_ref[:, D_CONV + D_HGRN:] = yr_ref[...]
    z = alpha * x_ref[...] + _dot(mix_ref[...], w_ref[...])
    o_ref[...] = _layer_norm_rows(z, g_ref[...], b_ref[...])


def _outproj(yc, yh, yr, w_stack, layer, x2d, ln_g, ln_b, alpha, tm):
    m, d = x2d.shape
    d_mix = D_CONV + D_HGRN + D_RET
    vec = pl.BlockSpec((1, d), lambda i: (0, 0))
    return pl.pallas_call(
        functools.partial(_outproj_kernel, alpha=alpha),
        name="out_proj_ln",
        grid=(m // tm,),
        in_specs=[pl.BlockSpec((tm, D_CONV), lambda i: (i, 0)),
                  pl.BlockSpec((tm, D_HGRN), lambda i: (i, 0)),
                  pl.BlockSpec((tm, D_RET), lambda i: (i, 0)),
                  pl.BlockSpec((None, d_mix, d), lambda i: (layer, 0, 0)),
                  pl.BlockSpec((tm, d), lambda i: (i, 0)),
                  vec, vec],
        out_specs=pl.BlockSpec((tm, d), lambda i: (i, 0)),
        out_shape=jax.ShapeDtypeStruct((m, d), F32),
        scratch_shapes=[pltpu.VMEM((tm, d_mix), BF16)],
        compiler_params=_params("parallel"),
    )(yc, yh, yr, w_stack, x2d, ln_g.reshape(1, d).astype(F32), ln_b.reshape(1, d).astype(F32))


def _ffn_kernel(x_ref, w1_ref, w2_ref, g_ref, b_ref, o_ref, xb_ref, acc_ref, *, alpha):
    f = pl.program_id(1)

    @pl.when(f == 0)
    def _():
        xb_ref[...] = x_ref[...].astype(BF16)
        acc_ref[...] = jnp.zeros(acc_ref.shape, F32)

    hid = jnp.maximum(_dot(xb_ref[...], w1_ref[...]), 0.0)
    acc_ref[...] += _dot((hid * hid).astype(BF16), w2_ref[...])

    @pl.when(f == pl.num_programs(1) - 1)
    def _():
        z = alpha * x_ref[...] + acc_ref[...]
        o_ref[...] = _layer_norm_rows(z, g_ref[...], b_ref[...])


def _ffn(x2d, w1_stack, w2_stack, layer, ln_g, ln_b, alpha, tm, tf):
    m, d = x2d.shape
    d_ff = w1_stack.shape[-1]
    vec = pl.BlockSpec((1, d), lambda i, f: (0, 0))
    return pl.pallas_call(
        functools.partial(_ffn_kernel, alpha=alpha),
        name="ffn_ln",
        grid=(m // tm, d_ff // tf),
        in_specs=[pl.BlockSpec((tm, d), lambda i, f: (i, 0)),
                  pl.BlockSpec((None, d, tf), lambda i, f: (layer, 0, f)),
                  pl.BlockSpec((None, tf, d), lambda i, f: (layer, f, 0)),
                  vec, vec],
        out_specs=pl.BlockSpec((tm, d), lambda i, f: (i, 0)),
        out_shape=jax.ShapeDtypeStruct((m, d), F32),
        scratch_shapes=[pltpu.VMEM((tm, d), BF16), pltpu.VMEM((tm, d), F32)],
        compiler_params=_params("parallel", "arbitrary"),
    )(x2d, w1_stack, w2_stack, ln_g.reshape(1, d).astype(F32), ln_b.reshape(1, d).astype(F32))


def kernel(x, positions, w_in, w_dw, b_dw, conv_ln_g, conv_ln_b, hgrn_lb, hgrn_norm_g, ret_gn_g, ret_gn_b,
           w_out, ln1_g, ln1_b, w_ff1, w_ff2, ln2_g, ln2_b):
    b, t, d = x.shape
    depth = w_in.shape[0]
    alpha = (2.0 * depth) ** 0.25
    m = b * t
    tt = min(512, t)
    tm = min(1024, m)
    tm_ffn = min(512, m)
    tn = 1024
    tf = 512

    conv_cols = 2 * D_CONV
    w_in_b = jnp.concatenate([w_in[:, :, conv_cols:], w_in[:, :, :conv_cols]], axis=-1).astype(BF16)
    w_out_b = w_out.astype(BF16)
    w_ff1_b = w_ff1.astype(BF16)
    w_ff2_b = w_ff2.astype(BF16)

    lower = _lower_bounds(hgrn_lb)
    cos_t, sin_t = _rope_tables(positions, tt)

    x2d = x.reshape(m, d).astype(F32)
    for l in range(depth):
        proj = _proj(x2d, w_in_b, l, tm, tn).reshape(b, t, -1)
        yc = _conv_mixer(proj, w_dw[l], b_dw[l], conv_ln_g[l], conv_ln_b[l], tt)
        yh = _hgrn_mixer(proj, lower[l], hgrn_norm_g[l], tt)
        yr = _ret_mixer(proj, cos_t, sin_t, ret_gn_g[l], ret_gn_b[l], tt)
        x2d = _outproj(yc.reshape(m, D_CONV), yh.reshape(m, D_HGRN), yr.reshape(m, D_RET),
                       w_out_b, l, x2d, ln1_g[l], ln1_b[l], alpha, tm_ffn)
        x2d = _ffn(x2d, w_ff1_b, w_ff2_b, l, ln2_g[l], ln2_b[l], alpha, tm_ffn, tf)
    return x2d.reshape(b, t, d).astype(x.dtype)
```

```python
import functools
import math

import numpy as np
import jax
import jax.numpy as jnp
from jax import lax
from jax.experimental import pallas as pl
from jax.experimental.pallas import tpu as pltpu

F32 = jnp.float32
BF16 = jnp.bfloat16

D_CONV = 512
D_HGRN = 768
D_RET = 768
CONV_WIDTH = 31
HEAD_DIM = 128
HGRN_HEADS = D_HGRN // HEAD_DIM
RET_HEADS = D_RET // HEAD_DIM
ROPE_BASE = 10000.0
LN_EPS = 1e-5

LANES = 128
SUBLANES = 8
HGRN_CHUNK = 64
HGRN_LEVELS = 6
HGRN_PAIRS = HGRN_HEADS // 2
HGRN_UNROLL = 2
PAIR = 2 * HEAD_DIM
RET_CHUNK = 128
CONV_HALO = 32
CONV_ROWS = 128
NORM_ROWS = 64

VMEM_LIMIT_BYTES = 56 * 1024 * 1024

HGRN_COL_BLOCK = 0
RET_COL_BLOCK = 4 * D_HGRN // D_RET
CONV_COL_BLOCK = (4 * D_HGRN + 4 * D_RET) // D_CONV


def _dot(a, b):
    return jnp.dot(a, b, preferred_element_type=F32)


def _dot_nt(a, b):
    return lax.dot_general(a, b, (((1,), (1,)), ((), ())), preferred_element_type=F32)


def _dot_tn(a, b):
    return lax.dot_general(a, b, (((0,), (0,)), ((), ())), preferred_element_type=F32)


def _params(*semantics):
    return pltpu.CompilerParams(dimension_semantics=semantics, vmem_limit_bytes=VMEM_LIMIT_BYTES)


def _layer_norm_rows(z, g, b):
    mu = jnp.mean(z, axis=-1, keepdims=True)
    zc = z - mu
    var = jnp.mean(zc * zc, axis=-1, keepdims=True)
    return zc * lax.rsqrt(var + LN_EPS) * g + b


def _lower_bounds_kernel(lb_ref, o_ref):
    depth = lb_ref.shape[0]
    rows = [lb_ref[l:l + 1, :] for l in range(depth)]
    m = functools.reduce(jnp.maximum, rows)
    ex = [jnp.exp(r - m) for r in rows]
    tot = functools.reduce(jnp.add, ex)
    p = [e / tot for e in ex]
    c = p[0]
    for l in range(depth):
        if l > 0:
            c = c + p[l]
        o_ref[l:l + 1, :] = jnp.clip(c - p[0], 0.0, 1.0 - 1e-6)


def _lower_bounds(hgrn_lb):
    return pl.pallas_call(
        _lower_bounds_kernel,
        name="lower_bounds",
        out_shape=jax.ShapeDtypeStruct(hgrn_lb.shape, F32),
    )(hgrn_lb.astype(F32))


def _rope_kernel(pos_ref, inv_ref, cos_ref, sin_ref):
    ang = pos_ref[...].astype(F32) * inv_ref[...]
    lane = lax.broadcasted_iota(jnp.int32, ang.shape, 1)
    s = jnp.sin(ang)
    cos_ref[...] = jnp.cos(ang)
    sin_ref[...] = jnp.where(lane < HEAD_DIM // 2, -s, s)


def _rope_tables(positions, tt):
    b, t = positions.shape
    half = HEAD_DIM // 2
    inv = ROPE_BASE ** (-jnp.arange(half, dtype=F32) / half)
    inv2 = jnp.concatenate([inv, inv]).reshape(1, HEAD_DIM)
    tab = jax.ShapeDtypeStruct((b, t, HEAD_DIM), F32)
    return pl.pallas_call(
        _rope_kernel,
        name="rope_tables",
        grid=(b, t // tt),
        in_specs=[pl.BlockSpec((None, tt, 1), lambda i, j: (i, j, 0)),
                  pl.BlockSpec((1, HEAD_DIM), lambda i, j: (0, 0))],
        out_specs=[pl.BlockSpec((None, tt, HEAD_DIM), lambda i, j: (i, j, 0))] * 2,
        out_shape=[tab, tab],
        compiler_params=_params("parallel", "parallel"),
    )(positions.reshape(b, t, 1), inv2)


def _proj_kernel(x_ref, w_ref, o_ref, xb_ref):
    @pl.when(pl.program_id(1) == 0)
    def _():
        xb_ref[...] = x_ref[...].astype(BF16)

    o_ref[...] = _dot(xb_ref[...], w_ref[...])


def _proj(x2d, w_stack, layer, tm, tn):
    m, k = x2d.shape
    n = w_stack.shape[-1]
    n_blocks = n // tn
    first = 2 * D_CONV // tn
    return pl.pallas_call(
        _proj_kernel,
        name="in_proj",
        grid=(m // tm, n_blocks),
        in_specs=[pl.BlockSpec((tm, k), lambda i, j: (i, 0)),
                  pl.BlockSpec((None, k, tn), lambda i, j: (layer, 0, (j + first) % n_blocks))],
        out_specs=pl.BlockSpec((tm, tn), lambda i, j: (i, j)),
        out_shape=jax.ShapeDtypeStruct((m, n), F32),
        scratch_shapes=[pltpu.VMEM((tm, k), BF16)],
        compiler_params=_params("parallel", "arbitrary"),
    )(x2d, w_stack)


def _conv_kernel(a_ref, g_ref, w_ref, b_ref, lng_ref, lnb_ref, o_ref, u_ref, y_ref, *, tt):
    @pl.when(pl.program_id(1) == 0)
    def _():
        u_ref[0:CONV_HALO, :] = jnp.zeros((CONV_HALO, D_CONV), F32)

    @pl.when(pl.program_id(1) > 0)
    def _():
        u_ref[0:CONV_HALO, :] = u_ref[tt:tt + CONV_HALO, :]

    u_ref[CONV_HALO:CONV_HALO + tt, :] = a_ref[...] * jax.nn.sigmoid(g_ref[...])

    first_tap = CONV_HALO - (CONV_WIDTH - 1)
    rows = min(CONV_ROWS, tt)
    win = rows + CONV_HALO
    for r0 in range(0, tt, rows):
        for c0 in range(0, D_CONV, LANES):
            acc = jnp.broadcast_to(b_ref[:, c0:c0 + LANES], (rows, LANES))
            for r in range(SUBLANES):
                window = u_ref[r0:r0 + win, c0:c0 + LANES]
                shifted = window if r == 0 else pltpu.roll(window, win - r, 0)
                for a in range(win // SUBLANES):
                    j = SUBLANES * a + r - first_tap
                    if 0 <= j < CONV_WIDTH:
                        acc = acc + w_ref[j:j + 1, c0:c0 + LANES] * shifted[SUBLANES * a:SUBLANES * a + rows]
            y_ref[r0:r0 + rows, c0:c0 + LANES] = acc

    nrows = min(NORM_ROWS, tt)
    for r0 in range(0, tt, nrows):
        v = _layer_norm_rows(y_ref[r0:r0 + nrows, :], lng_ref[...], lnb_ref[...])
        o_ref[r0:r0 + nrows, :] = (v * jax.nn.sigmoid(v)).astype(o_ref.dtype)


def _conv_mixer(proj, w_dw, b_dw, ln_g, ln_b, tt):
    b, t, _ = proj.shape
    row = lambda v: v.reshape(1, D_CONV).astype(F32)
    vec = pl.BlockSpec((1, D_CONV), lambda i, j: (0, 0))
    blk = lambda n: pl.BlockSpec((None, tt, D_CONV), lambda i, j: (i, j, CONV_COL_BLOCK + n))
    return pl.pallas_call(
        functools.partial(_conv_kernel, tt=tt),
        name="conv_mixer",
        grid=(b, t // tt),
        in_specs=[blk(0), blk(1),
                  pl.BlockSpec((CONV_WIDTH, D_CONV), lambda i, j: (0, 0)),
                  vec, vec, vec],
        out_specs=pl.BlockSpec((None, tt, D_CONV), lambda i, j: (i, j, 0)),
        out_shape=jax.ShapeDtypeStruct((b, t, D_CONV), BF16),
        scratch_shapes=[pltpu.VMEM((CONV_HALO + tt, D_CONV), F32),
                        pltpu.VMEM((tt, D_CONV), F32)],
        compiler_params=_params("parallel", "arbitrary"),
    )(proj, proj, w_dw.astype(F32), row(b_dw), row(ln_g), row(ln_b))


def _hgrn_tables():
    c = HGRN_CHUNK
    mat = np.zeros((HGRN_LEVELS, c, c), np.float32)
    for l in range(1, HGRN_LEVELS):
        half = 1 << l
        for t in range(c):
            mid = ((t >> (l + 1)) << (l + 1)) + half - 1
            if t > mid:
                mat[l - 1, t, mid + 1:t + 1] = 1.0
            else:
                mat[l - 1, t, t + 1:mid + 1] = 1.0
    mat[HGRN_LEVELS - 1] = np.tril(np.ones((c, c), np.float32))
    mat = mat.reshape(-1, c)
    t = np.arange(c)[:, None]
    s = np.arange(c)[None, :]
    x = t ^ s
    lvl = np.zeros((c, c), np.int32)
    for l in range(1, HGRN_LEVELS):
        lvl += (x >= (1 << l)).astype(np.int32)
    lvl = np.where(t == s, HGRN_LEVELS, lvl)
    lvl = np.where(t < s, HGRN_LEVELS + 1, lvl)
    return np.concatenate([mat, mat, mat], axis=1), np.concatenate([lvl, lvl], axis=1).astype(np.int32)


def _hgrn_kernel(q_ref, f_ref, i_ref, g_ref, lb_ref, ng_ref, mat_ref, lvl_ref, o_ref,
                 qs_ref, ks_ref, lf_ref, e_ref, kb_ref, ib_ref, st_ref, *, tt):
    c = HGRN_CHUNK

    @pl.when(pl.program_id(1) == 0)
    def _():
        st_ref[...] = jnp.zeros(st_ref.shape, F32)
        kb_ref[...] = jnp.zeros(kb_ref.shape, BF16)
        ib_ref[...] = jnp.zeros(ib_ref.shape, BF16)

    q = q_ref[...]
    qs_ref[...] = q * jax.nn.sigmoid(q)
    z = f_ref[...]
    ez = jnp.exp(-jnp.abs(z))
    inv = 1.0 / (1.0 + ez)
    log_sig = jnp.minimum(z, 0.0) - jnp.log(1.0 + ez)
    lb = lb_ref[...]
    f_pos = lb + (1.0 - lb) * (jnp.where(z >= 0, 1.0, ez) * inv)
    lf_ref[...] = jnp.where(lb > 0, jnp.log(f_pos), log_sig)
    ks_ref[...] = (1.0 - lb) * (jnp.where(z >= 0, ez, 1.0) * inv)

    lvl = lvl_ref[...]
    odd_row = (lax.broadcasted_iota(jnp.int32, (c, PAIR), 0) & 1) == 1

    tasks = [(u, p) for u in range(HGRN_UNROLL) for p in range(HGRN_PAIRS)]

    def chunk(ci, carry):
        base = ci * (HGRN_UNROLL * c)
        rows = [pl.ds(pl.multiple_of(base + u * c, c), c) for u in range(HGRN_UNROLL)]
        lanes = [slice(p * PAIR, (p + 1) * PAIR) for p in range(HGRN_PAIRS)]
        slot = {t: n for n, t in enumerate(tasks)}

        lf, sums = {}, {}
        for t in tasks:
            u, p = t
            lf[t] = lf_ref[rows[u], lanes[p]]
            hi = lf[t].astype(BF16)
            r1 = lf[t] - hi.astype(F32)
            mid = r1.astype(BF16)
            lo = (r1 - mid.astype(F32)).astype(BF16)
            sums[t] = _dot(mat_ref[...], jnp.concatenate([hi, mid, lo], axis=0))

        q2, k2, i2, erem = {}, {}, {}, {}
        for t in tasks:
            u, p = t
            e_ref[slot[t]] = jnp.exp(sums[t])
            cum = sums[t][(HGRN_LEVELS - 1) * c:HGRN_LEVELS * c]
            erem[t] = jnp.exp(cum[c - 1:c, :] - cum)
            q2[t] = qs_ref[rows[u], lanes[p]]
            k2[t] = ks_ref[rows[u], lanes[p]]
            i2[t] = i_ref[rows[u], lanes[p]].astype(BF16)
            ib_ref[slot[t], 0:c, 0:HEAD_DIM] = i2[t][:, 0:HEAD_DIM]
            ib_ref[slot[t], c:2 * c, HEAD_DIM:PAIR] = i2[t][:, HEAD_DIM:PAIR]

        def scores(t, level, el):
            kt = (k2[t] * el).astype(BF16) if el is not None else k2[t].astype(BF16)
            qt = (q2[t] * el).astype(BF16) if el is not None else q2[t].astype(BF16)
            kb_ref[slot[t], level, 0:c, 0:HEAD_DIM] = kt[:, 0:HEAD_DIM]
            kb_ref[slot[t], level, c:2 * c, HEAD_DIM:PAIR] = kt[:, HEAD_DIM:PAIR]
            return _dot_nt(qt, kb_ref[slot[t], level])

        att = {t: jnp.where(lvl == HGRN_LEVELS, scores(t, HGRN_LEVELS, None), 0.0) for t in tasks}
        for t in tasks:
            att[t] = jnp.where(lvl == 0, scores(t, 0, jnp.exp(jnp.where(odd_row, lf[t], 0.0))), att[t])
        for l in range(1, HGRN_LEVELS):
            for t in tasks:
                att[t] = jnp.where(lvl == l, scores(t, l, e_ref[slot[t], (l - 1) * c:l * c, :]), att[t])

        o2 = {t: _dot(att[t].astype(BF16), ib_ref[slot[t]]) for t in tasks}

        for t in tasks:
            u, p = t
            ecum = e_ref[slot[t], (HGRN_LEVELS - 1) * c:HGRN_LEVELS * c, :]
            glast = e_ref[slot[t], HGRN_LEVELS * c - 1:HGRN_LEVELS * c, :]
            qc = (q2[t] * ecum).astype(BF16)
            kr = (k2[t] * erem[t]).astype(BF16)
            gate = ng_ref[:, lanes[p]] * jax.nn.sigmoid(g_ref[rows[u], lanes[p]])
            for hh in range(2):
                h = 2 * p + hh
                hl = slice(hh * HEAD_DIM, (hh + 1) * HEAD_DIM)
                st = st_ref[h]
                o = o2[t][:, hl] + _dot_nt(qc[:, hl], st.astype(BF16))
                st_ref[h] = st * glast[:, hl] + _dot_tn(i2[t][:, hl], kr[:, hl])
                o = o * lax.rsqrt(jnp.mean(o * o, axis=-1, keepdims=True) + LN_EPS)
                o_ref[rows[u], h * HEAD_DIM:(h + 1) * HEAD_DIM] = (o * gate[:, hl]).astype(o_ref.dtype)
        return carry

    lax.fori_loop(0, tt // (HGRN_UNROLL * c), chunk, 0)


def _hgrn_mixer(proj, lower, norm_g, tt):
    b, t, _ = proj.shape
    mat, lvl = _hgrn_tables()
    c = HGRN_CHUNK
    blk = lambda n: pl.BlockSpec((None, tt, D_HGRN), lambda i, j: (i, j, HGRN_COL_BLOCK + n))
    vec = pl.BlockSpec((1, D_HGRN), lambda i, j: (0, 0))
    return pl.pallas_call(
        functools.partial(_hgrn_kernel, tt=tt),
        name="hgrn_mixer",
        grid=(b, t // tt),
        in_specs=[blk(0), blk(1), blk(2), blk(3), vec, vec,
                  pl.BlockSpec(mat.shape, lambda i, j: (0, 0)),
                  pl.BlockSpec(lvl.shape, lambda i, j: (0, 0))],
        out_specs=pl.BlockSpec((None, tt, D_HGRN), lambda i, j: (i, j, 0)),
        out_shape=jax.ShapeDtypeStruct((b, t, D_HGRN), BF16),
        scratch_shapes=[pltpu.VMEM((tt, D_HGRN), F32),
                        pltpu.VMEM((tt, D_HGRN), F32),
                        pltpu.VMEM((tt, D_HGRN), F32),
                        pltpu.VMEM((HGRN_UNROLL * HGRN_PAIRS, HGRN_LEVELS * c, PAIR), F32),
                        pltpu.VMEM((HGRN_UNROLL * HGRN_PAIRS, HGRN_LEVELS + 1, 2 * c, PAIR), BF16),
                        pltpu.VMEM((HGRN_UNROLL * HGRN_PAIRS, 2 * c, PAIR), BF16),
                        pltpu.VMEM((HGRN_HEADS, HEAD_DIM, HEAD_DIM), F32)],
        compiler_params=_params("parallel", "arbitrary"),
    )(proj, proj, proj, proj, lower.reshape(1, D_HGRN), norm_g.reshape(1, D_HGRN).astype(F32),
      jnp.asarray(mat, BF16), jnp.asarray(lvl))


def _ret_kernel(q_ref, k_ref, v_ref, g_ref, cos_ref, sin_ref, gg_ref, gb_ref, o_ref,
                intra_ref, qdec_ref, kdec_ref, st_ref, *, tt, c):
    log_gamma = [math.log1p(-(2.0 ** (-5.0 - h))) for h in range(RET_HEADS)]

    @pl.when(pl.program_id(1) == 0)
    def _():
        st_ref[...] = jnp.zeros(st_ref.shape, F32)
        row = lax.broadcasted_iota(jnp.int32, (c, c), 0)
        col = lax.broadcasted_iota(jnp.int32, (c, c), 1)
        diff = (row - col).astype(F32)
        rowl = lax.broadcasted_iota(jnp.int32, (c, HEAD_DIM), 0).astype(F32)
        for h in range(RET_HEADS):
            intra_ref[h] = jnp.where(diff >= 0, jnp.exp(log_gamma[h] * jnp.maximum(diff, 0.0)), 0.0)
            qdec_ref[h] = jnp.exp(log_gamma[h] * (rowl + 1.0))
            kdec_ref[h] = jnp.exp(log_gamma[h] * (c - 1.0 - rowl))

    k_scale = HEAD_DIM ** -0.5

    def chunk(ci, carry):
        r0 = pl.multiple_of(ci * c, c)
        rows = pl.ds(r0, c)
        cs = cos_ref[rows, :]
        sn = sin_ref[rows, :]
        heads = range(RET_HEADS)
        cols = [slice(h * HEAD_DIM, (h + 1) * HEAD_DIM) for h in heads]
        qr, kr, vh, scores = [], [], [], []
        for h in heads:
            qh = q_ref[rows, cols[h]]
            kh = k_ref[rows, cols[h]]
            qr.append(qh * cs + pltpu.roll(qh, HEAD_DIM // 2, 1) * sn)
            kr.append((kh * cs + pltpu.roll(kh, HEAD_DIM // 2, 1) * sn) * k_scale)
            vh.append(v_ref[rows, cols[h]].astype(BF16))
            scores.append(_dot_nt(qr[h].astype(BF16), kr[h].astype(BF16)))
        outs = []
        for h in heads:
            st = st_ref[h]
            masked = (scores[h] * intra_ref[h]).astype(BF16)
            outs.append(_dot(masked, vh[h]) + _dot((qr[h] * qdec_ref[h]).astype(BF16), st.astype(BF16)))
            st_ref[h] = math.exp(log_gamma[h] * c) * st + _dot_tn((kr[h] * kdec_ref[h]).astype(BF16), vh[h])
        for h in heads:
            o = outs[h]
            mu = jnp.mean(o, axis=-1, keepdims=True)
            oc = o - mu
            var = jnp.mean(oc * oc, axis=-1, keepdims=True)
            on = oc * lax.rsqrt(var + LN_EPS) * gg_ref[:, cols[h]] + gb_ref[:, cols[h]]
            gh = g_ref[rows, cols[h]]
            o_ref[rows, cols[h]] = (gh * jax.nn.sigmoid(gh) * on).astype(o_ref.dtype)
        return carry

    lax.fori_loop(0, tt // c, chunk, 0)


def _ret_mixer(proj, cos_t, sin_t, gn_g, gn_b, tt):
    b, t, _ = proj.shape
    c = min(RET_CHUNK, tt)
    blk = lambda n: pl.BlockSpec((None, tt, D_RET), lambda i, j: (i, j, RET_COL_BLOCK + n))
    tab = pl.BlockSpec((None, tt, HEAD_DIM), lambda i, j: (i, j, 0))
    vec = pl.BlockSpec((1, D_RET), lambda i, j: (0, 0))
    return pl.pallas_call(
        functools.partial(_ret_kernel, tt=tt, c=c),
        name="ret_mixer",
        grid=(b, t // tt),
        in_specs=[blk(0), blk(1), blk(2), blk(3), tab, tab, vec, vec],
        out_specs=pl.BlockSpec((None, tt, D_RET), lambda i, j: (i, j, 0)),
        out_shape=jax.ShapeDtypeStruct((b, t, D_RET), BF16),
        scratch_shapes=[pltpu.VMEM((RET_HEADS, c, c), F32),
                        pltpu.VMEM((RET_HEADS, c, HEAD_DIM), F32),
                        pltpu.VMEM((RET_HEADS, c, HEAD_DIM), F32),
                        pltpu.VMEM((RET_HEADS, HEAD_DIM, HEAD_DIM), F32)],
        compiler_params=_params("parallel", "arbitrary"),
    )(proj, proj, proj, proj, cos_t, sin_t, gn_g.reshape(1, D_RET).astype(F32), gn_b.reshape(1, D_RET).astype(F32))


def _outproj_kernel(yc_ref, yh_ref, yr_ref, w_ref, x_ref, g_ref, b_ref, o_ref, mix_ref, *, alpha):
    mix_ref[:, 0:D_CONV] = yc_ref[...]
    mix_ref[:, D_CONV:D_CONV + D_HGRN] = yh_ref[...]
    mix_ref[:, D_CONV + D_HGRN:] = yr_ref[...]
    z = alpha * x_ref[...] + _dot(mix_ref[...], w_ref[...])
    o_ref[...] = _layer_norm_rows(z, g_ref[...], b_ref[...])


def _outproj(yc, yh, yr, w_stack, layer, x2d, ln_g, ln_b, alpha, tm):
    m, d = x2d.shape
    d_mix = D_CONV + D_HGRN + D_RET
    vec = pl.BlockSpec((1, d), lambda i: (0, 0))
    return pl.pallas_call(
        functools.partial(_outproj_kernel, alpha=alpha),
        name="out_proj_ln",
        grid=(m // tm,),
        in_specs=[pl.BlockSpec((tm, D_CONV), lambda i: (i, 0)),
                  pl.BlockSpec((tm, D_HGRN), lambda i: (i, 0)),
                  pl.BlockSpec((tm, D_RET), lambda i: (i, 0)),
                  pl.BlockSpec((None, d_mix, d), lambda i: (layer, 0, 0)),
                  pl.BlockSpec((tm, d), lambda i: (i, 0)),
                  vec, vec],
        out_specs=pl.BlockSpec((tm, d), lambda i: (i, 0)),
        out_shape=jax.ShapeDtypeStruct((m, d), F32),
        scratch_shapes=[pltpu.VMEM((tm, d_mix), BF16)],
        compiler_params=_params("parallel"),
    )(yc, yh, yr, w_stack, x2d, ln_g.reshape(1, d).astype(F32), ln_b.reshape(1, d).astype(F32))


def _ffn_kernel(x_ref, w1_ref, w2_ref, g_ref, b_ref, o_ref, xb_ref, *, alpha):
    f = pl.program_id(1)

    @pl.when(f == 0)
    def _():
        xb_ref[...] = x_ref[...].astype(BF16)
        o_ref[...] = jnp.zeros(o_ref.shape, F32)

    hid = jnp.maximum(_dot(xb_ref[...], w1_ref[...]), 0.0)
    o_ref[...] += _dot((hid * hid).astype(BF16), w2_ref[...])

    @pl.when(f == pl.num_programs(1) - 1)
    def _():
        z = alpha * x_ref[...] + o_ref[...]
        o_ref[...] = _layer_norm_rows(z, g_ref[...], b_ref[...])


def _ffn(x2d, w1_stack, w2_stack, layer, ln_g, ln_b, alpha, tm, tf):
    m, d = x2d.shape
    d_ff = w1_stack.shape[-1]
    vec = pl.BlockSpec((1, d), lambda i, f: (0, 0))
    return pl.pallas_call(
        functools.partial(_ffn_kernel, alpha=alpha),
        name="ffn_ln",
        grid=(m // tm, d_ff // tf),
        in_specs=[pl.BlockSpec((tm, d), lambda i, f: (i, 0), pipeline_mode=pl.Buffered(1)),
                  pl.BlockSpec((None, d, tf), lambda i, f: (layer, 0, f)),
                  pl.BlockSpec((None, tf, d), lambda i, f: (layer, f, 0)),
                  vec, vec],
        out_specs=pl.BlockSpec((tm, d), lambda i, f: (i, 0), pipeline_mode=pl.Buffered(1)),
        out_shape=jax.ShapeDtypeStruct((m, d), F32),
        scratch_shapes=[pltpu.VMEM((tm, d), BF16)],
        compiler_params=_params("parallel", "arbitrary"),
    )(x2d, w1_stack, w2_stack, ln_g.reshape(1, d).astype(F32), ln_b.reshape(1, d).astype(F32))


def kernel(x, positions, w_in, w_dw, b_dw, conv_ln_g, conv_ln_b, hgrn_lb, hgrn_norm_g, ret_gn_g, ret_gn_b,
           w_out, ln1_g, ln1_b, w_ff1, w_ff2, ln2_g, ln2_b):
    b, t, d = x.shape
    depth = w_in.shape[0]
    alpha = (2.0 * depth) ** 0.25
    m = b * t
    tt = min(512, t)
    tm = min(1024, m)
    tm_out = min(512, m)
    tn = 2 * D_CONV
    tf = 1024

    w_in_b = w_in.astype(BF16)
    w_out_b = w_out.astype(BF16)
    w_ff1_b = w_ff1.astype(BF16)
    w_ff2_b = w_ff2.astype(BF16)

    lower = _lower_bounds(hgrn_lb)
    cos_t, sin_t = _rope_tables(positions, tt)

    x2d = x.reshape(m, d).astype(F32)
    for l in range(depth):
        proj = _proj(x2d, w_in_b, l, tm, tn).reshape(b, t, -1)
        yc = _conv_mixer(proj, w_dw[l], b_dw[l], conv_ln_g[l], conv_ln_b[l], tt)
        yh = _hgrn_mixer(proj, lower[l], hgrn_norm_g[l], tt)
        yr = _ret_mixer(proj, cos_t, sin_t, ret_gn_g[l], ret_gn_b[l], tt)
        x2d = _outproj(yc.reshape(m, D_CONV), yh.reshape(m, D_HGRN), yr.reshape(m, D_RET),
                       w_out_b, l, x2d, ln1_g[l], ln1_b[l], alpha, tm_out)
        x2d = _ffn(x2d, w_ff1_b, w_ff2_b, l, ln2_g[l], ln2_b[l], alpha, tm, tf)
    return x2d.reshape(b, t, d).astype(x.dtype)
```

```python
import functools
import math

import numpy as np
import jax
import jax.numpy as jnp
from jax import lax
from jax.experimental import pallas as pl
from jax.experimental.pallas import tpu as pltpu

F32 = jnp.float32
BF16 = jnp.bfloat16

D_CONV = 512
D_HGRN = 768
D_RET = 768
CONV_WIDTH = 31
HEAD_DIM = 128
HGRN_HEADS = D_HGRN // HEAD_DIM
RET_HEADS = D_RET // HEAD_DIM
ROPE_BASE = 10000.0
LN_EPS = 1e-5

LANES = 128
SUBLANES = 8
HGRN_CHUNK = 64
HGRN_LEVELS = 6
HGRN_PAIRS = HGRN_HEADS // 2
HGRN_UNROLL = 4
PAIR = 2 * HEAD_DIM
RET_CHUNK = 128
RET_UNROLL = 1
CONV_HALO = 32
CONV_ROWS = 128
NORM_ROWS = 64

VMEM_LIMIT_BYTES = 56 * 1024 * 1024

HGRN_COL_BLOCK = 0
RET_COL_BLOCK = 4 * D_HGRN // D_RET
CONV_COL_BLOCK = (4 * D_HGRN + 4 * D_RET) // D_CONV


def _dot(a, b):
    return jnp.dot(a, b, preferred_element_type=F32)


def _dot_nt(a, b):
    return lax.dot_general(a, b, (((1,), (1,)), ((), ())), preferred_element_type=F32)


def _dot_tn(a, b):
    return lax.dot_general(a, b, (((0,), (0,)), ((), ())), preferred_element_type=F32)


def _chunks_per_trip(tt, chunk, want):
    return max(u for u in range(1, want + 1) if tt % (u * chunk) == 0)


def _params(*semantics):
    return pltpu.CompilerParams(dimension_semantics=semantics, vmem_limit_bytes=VMEM_LIMIT_BYTES)


def _layer_norm_rows(z, g, b):
    mu = jnp.mean(z, axis=-1, keepdims=True)
    zc = z - mu
    var = jnp.mean(zc * zc, axis=-1, keepdims=True)
    return zc * lax.rsqrt(var + LN_EPS) * g + b


def _lower_bounds_kernel(lb_ref, o_ref):
    depth = lb_ref.shape[0]
    rows = [lb_ref[l:l + 1, :] for l in range(depth)]
    m = functools.reduce(jnp.maximum, rows)
    ex = [jnp.exp(r - m) for r in rows]
    tot = functools.reduce(jnp.add, ex)
    p = [e / tot for e in ex]
    c = p[0]
    for l in range(depth):
        if l > 0:
            c = c + p[l]
        o_ref[l:l + 1, :] = jnp.clip(c - p[0], 0.0, 1.0 - 1e-6)


def _lower_bounds(hgrn_lb):
    return pl.pallas_call(
        _lower_bounds_kernel,
        name="lower_bounds",
        out_shape=jax.ShapeDtypeStruct(hgrn_lb.shape, F32),
    )(hgrn_lb.astype(F32))


def _rope_kernel(pos_ref, inv_ref, cos_ref, sin_ref):
    ang = pos_ref[...].astype(F32) * inv_ref[...]
    lane = lax.broadcasted_iota(jnp.int32, ang.shape, 1)
    s = jnp.sin(ang)
    cos_ref[...] = jnp.cos(ang)
    sin_ref[...] = jnp.where(lane < HEAD_DIM // 2, -s, s)


def _rope_tables(positions, tt):
    b, t = positions.shape
    half = HEAD_DIM // 2
    inv = ROPE_BASE ** (-jnp.arange(half, dtype=F32) / half)
    inv2 = jnp.concatenate([inv, inv]).reshape(1, HEAD_DIM)
    tab = jax.ShapeDtypeStruct((b, t, HEAD_DIM), F32)
    return pl.pallas_call(
        _rope_kernel,
        name="rope_tables",
        grid=(b, t // tt),
        in_specs=[pl.BlockSpec((None, tt, 1), lambda i, j: (i, j, 0)),
                  pl.BlockSpec((1, HEAD_DIM), lambda i, j: (0, 0))],
        out_specs=[pl.BlockSpec((None, tt, HEAD_DIM), lambda i, j: (i, j, 0))] * 2,
        out_shape=[tab, tab],
        compiler_params=_params("parallel", "parallel"),
    )(positions.reshape(b, t, 1), inv2)


def _proj_kernel(x_ref, w_ref, o_ref, xb_ref):
    @pl.when(pl.program_id(1) == 0)
    def _():
        xb_ref[...] = x_ref[...].astype(BF16)

    o_ref[...] = _dot(xb_ref[...], w_ref[...])


def _proj(x2d, w_stack, layer, tm, tn):
    m, k = x2d.shape
    n = w_stack.shape[-1]
    n_blocks = n // tn
    first = 2 * D_CONV // tn
    return pl.pallas_call(
        _proj_kernel,
        name="in_proj",
        grid=(m // tm, n_blocks),
        in_specs=[pl.BlockSpec((tm, k), lambda i, j: (i, 0)),
                  pl.BlockSpec((None, k, tn), lambda i, j: (layer, 0, (j + first) % n_blocks))],
        out_specs=pl.BlockSpec((tm, tn), lambda i, j: (i, j)),
        out_shape=jax.ShapeDtypeStruct((m, n), F32),
        scratch_shapes=[pltpu.VMEM((tm, k), BF16)],
        compiler_params=_params("parallel", "arbitrary"),
    )(x2d, w_stack)


def _conv_kernel(a_ref, g_ref, w_ref, b_ref, lng_ref, lnb_ref, o_ref, u_ref, y_ref, *, tt):
    @pl.when(pl.program_id(1) == 0)
    def _():
        u_ref[0:CONV_HALO, :] = jnp.zeros((CONV_HALO, D_CONV), F32)

    @pl.when(pl.program_id(1) > 0)
    def _():
        u_ref[0:CONV_HALO, :] = u_ref[tt:tt + CONV_HALO, :]

    u_ref[CONV_HALO:CONV_HALO + tt, :] = a_ref[...] * jax.nn.sigmoid(g_ref[...])

    first_tap = CONV_HALO - (CONV_WIDTH - 1)
    rows = min(CONV_ROWS, tt)
    win = rows + CONV_HALO
    for r0 in range(0, tt, rows):
        for c0 in range(0, D_CONV, LANES):
            acc = jnp.broadcast_to(b_ref[:, c0:c0 + LANES], (rows, LANES))
            for r in range(SUBLANES):
                window = u_ref[r0:r0 + win, c0:c0 + LANES]
                shifted = window if r == 0 else pltpu.roll(window, win - r, 0)
                for a in range(win // SUBLANES):
                    j = SUBLANES * a + r - first_tap
                    if 0 <= j < CONV_WIDTH:
                        acc = acc + w_ref[j:j + 1, c0:c0 + LANES] * shifted[SUBLANES * a:SUBLANES * a + rows]
            y_ref[r0:r0 + rows, c0:c0 + LANES] = acc

    nrows = min(NORM_ROWS, tt)
    for r0 in range(0, tt, nrows):
        v = _layer_norm_rows(y_ref[r0:r0 + nrows, :], lng_ref[...], lnb_ref[...])
        o_ref[r0:r0 + nrows, :] = (v * jax.nn.sigmoid(v)).astype(o_ref.dtype)


def _conv_mixer(proj, w_dw, b_dw, ln_g, ln_b, tt):
    b, t, _ = proj.shape
    row = lambda v: v.reshape(1, D_CONV).astype(F32)
    vec = pl.BlockSpec((1, D_CONV), lambda i, j: (0, 0))
    blk = lambda n: pl.BlockSpec((None, tt, D_CONV), lambda i, j: (i, j, CONV_COL_BLOCK + n))
    return pl.pallas_call(
        functools.partial(_conv_kernel, tt=tt),
        name="conv_mixer",
        grid=(b, t // tt),
        in_specs=[blk(0), blk(1),
                  pl.BlockSpec((CONV_WIDTH, D_CONV), lambda i, j: (0, 0)),
                  vec, vec, vec],
        out_specs=pl.BlockSpec((None, tt, D_CONV), lambda i, j: (i, j, 0)),
        out_shape=jax.ShapeDtypeStruct((b, t, D_CONV), BF16),
        scratch_shapes=[pltpu.VMEM((CONV_HALO + tt, D_CONV), F32),
                        pltpu.VMEM((tt, D_CONV), F32)],
        compiler_params=_params("parallel", "arbitrary"),
    )(proj, proj, w_dw.astype(F32), row(b_dw), row(ln_g), row(ln_b))


def _hgrn_tables():
    c = HGRN_CHUNK
    mat = np.zeros((HGRN_LEVELS, c, c), np.float32)
    for l in range(1, HGRN_LEVELS):
        half = 1 << l
        for t in range(c):
            mid = ((t >> (l + 1)) << (l + 1)) + half - 1
            if t > mid:
                mat[l - 1, t, mid + 1:t + 1] = 1.0
            else:
                mat[l - 1, t, t + 1:mid + 1] = 1.0
    mat[HGRN_LEVELS - 1] = np.tril(np.ones((c, c), np.float32))
    mat = mat.reshape(-1, c)
    t = np.arange(c)[:, None]
    s = np.arange(c)[None, :]
    x = t ^ s
    lvl = np.zeros((c, c), np.int32)
    for l in range(1, HGRN_LEVELS):
        lvl += (x >= (1 << l)).astype(np.int32)
    lvl = np.where(t == s, HGRN_LEVELS, lvl)
    lvl = np.where(t < s, HGRN_LEVELS + 1, lvl)
    return np.concatenate([mat, mat, mat], axis=1), np.concatenate([lvl, lvl], axis=1).astype(np.int32)


def _hgrn_kernel(q_ref, f_ref, i_ref, g_ref, lb_ref, ng_ref, mat_ref, lvl_ref, o_ref,
                 qs_ref, ks_ref, lf_ref, e_ref, kb_ref, ib_ref, st_ref, *, tt):
    c = HGRN_CHUNK

    @pl.when(pl.program_id(1) == 0)
    def _():
        st_ref[...] = jnp.zeros(st_ref.shape, F32)
        kb_ref[...] = jnp.zeros(kb_ref.shape, BF16)
        ib_ref[...] = jnp.zeros(ib_ref.shape, BF16)

    q = q_ref[...]
    qs_ref[...] = q * jax.nn.sigmoid(q)
    z = f_ref[...]
    ez = jnp.exp(-jnp.abs(z))
    inv = 1.0 / (1.0 + ez)
    log_sig = jnp.minimum(z, 0.0) - jnp.log(1.0 + ez)
    lb = lb_ref[...]
    f_pos = lb + (1.0 - lb) * (jnp.where(z >= 0, 1.0, ez) * inv)
    lf_ref[...] = jnp.where(lb > 0, jnp.log(f_pos), log_sig)
    ks_ref[...] = (1.0 - lb) * (jnp.where(z >= 0, ez, 1.0) * inv)

    lvl = lvl_ref[...]
    odd_row = (lax.broadcasted_iota(jnp.int32, (c, PAIR), 0) & 1) == 1

    unroll = _chunks_per_trip(tt, c, HGRN_UNROLL)
    tasks = [(u, p) for u in range(unroll) for p in range(HGRN_PAIRS)]

    def chunk(ci, carry):
        base = ci * (unroll * c)
        rows = [pl.ds(pl.multiple_of(base + u * c, c), c) for u in range(unroll)]
        lanes = [slice(p * PAIR, (p + 1) * PAIR) for p in range(HGRN_PAIRS)]
        slot = {t: n for n, t in enumerate(tasks)}

        lf, sums = {}, {}
        for t in tasks:
            u, p = t
            lf[t] = lf_ref[rows[u], lanes[p]]
            hi = lf[t].astype(BF16)
            r1 = lf[t] - hi.astype(F32)
            mid = r1.astype(BF16)
            lo = (r1 - mid.astype(F32)).astype(BF16)
            sums[t] = _dot(mat_ref[...], jnp.concatenate([hi, mid, lo], axis=0))

        q2, k2, i2, erem = {}, {}, {}, {}
        for t in tasks:
            u, p = t
            e_ref[slot[t]] = jnp.exp(sums[t])
            cum = sums[t][(HGRN_LEVELS - 1) * c:HGRN_LEVELS * c]
            erem[t] = jnp.exp(cum[c - 1:c, :] - cum)
            q2[t] = qs_ref[rows[u], lanes[p]]
            k2[t] = ks_ref[rows[u], lanes[p]]
            i2[t] = i_ref[rows[u], lanes[p]].astype(BF16)
            ib_ref[slot[t], 0:c, 0:HEAD_DIM] = i2[t][:, 0:HEAD_DIM]
            ib_ref[slot[t], c:2 * c, HEAD_DIM:PAIR] = i2[t][:, HEAD_DIM:PAIR]

        def scores(t, level, el):
            kt = (k2[t] * el).astype(BF16) if el is not None else k2[t].astype(BF16)
            qt = (q2[t] * el).astype(BF16) if el is not None else q2[t].astype(BF16)
            kb_ref[slot[t], level, 0:c, 0:HEAD_DIM] = kt[:, 0:HEAD_DIM]
            kb_ref[slot[t], level, c:2 * c, HEAD_DIM:PAIR] = kt[:, HEAD_DIM:PAIR]
            return _dot_nt(qt, kb_ref[slot[t], level])

        att = {t: jnp.where(lvl == HGRN_LEVELS, scores(t, HGRN_LEVELS, None), 0.0) for t in tasks}
        for t in tasks:
            att[t] = jnp.where(lvl == 0, scores(t, 0, jnp.exp(jnp.where(odd_row, lf[t], 0.0))), att[t])
        for l in range(1, HGRN_LEVELS):
            for t in tasks:
                att[t] = jnp.where(lvl == l, scores(t, l, e_ref[slot[t], (l - 1) * c:l * c, :]), att[t])

        o2 = {t: _dot(att[t].astype(BF16), ib_ref[slot[t]]) for t in tasks}

        for t in tasks:
            u, p = t
            ecum = e_ref[slot[t], (HGRN_LEVELS - 1) * c:HGRN_LEVELS * c, :]
            glast = e_ref[slot[t], HGRN_LEVELS * c - 1:HGRN_LEVELS * c, :]
            qc = (q2[t] * ecum).astype(BF16)
            kr = (k2[t] * erem[t]).astype(BF16)
            gate = ng_ref[:, lanes[p]] * jax.nn.sigmoid(g_ref[rows[u], lanes[p]])
            for hh in range(2):
                h = 2 * p + hh
                hl = slice(hh * HEAD_DIM, (hh + 1) * HEAD_DIM)
                st = st_ref[h]
                o = o2[t][:, hl] + _dot_nt(qc[:, hl], st.astype(BF16))
                st_ref[h] = st * glast[:, hl] + _dot_tn(i2[t][:, hl], kr[:, hl])
                o = o * lax.rsqrt(jnp.mean(o * o, axis=-1, keepdims=True) + LN_EPS)
                o_ref[rows[u], h * HEAD_DIM:(h + 1) * HEAD_DIM] = (o * gate[:, hl]).astype(o_ref.dtype)
        return carry

    lax.fori_loop(0, tt // (unroll * c), chunk, 0)


def _hgrn_mixer(proj, lower, norm_g, tt):
    b, t, _ = proj.shape
    mat, lvl = _hgrn_tables()
    c = HGRN_CHUNK
    blk = lambda n: pl.BlockSpec((None, tt, D_HGRN), lambda i, j: (i, j, HGRN_COL_BLOCK + n))
    vec = pl.BlockSpec((1, D_HGRN), lambda i, j: (0, 0))
    return pl.pallas_call(
        functools.partial(_hgrn_kernel, tt=tt),
        name="hgrn_mixer",
        grid=(b, t // tt),
        in_specs=[blk(0), blk(1), blk(2), blk(3), vec, vec,
                  pl.BlockSpec(mat.shape, lambda i, j: (0, 0)),
                  pl.BlockSpec(lvl.shape, lambda i, j: (0, 0))],
        out_specs=pl.BlockSpec((None, tt, D_HGRN), lambda i, j: (i, j, 0)),
        out_shape=jax.ShapeDtypeStruct((b, t, D_HGRN), BF16),
        scratch_shapes=[pltpu.VMEM((tt, D_HGRN), F32),
                        pltpu.VMEM((tt, D_HGRN), F32),
                        pltpu.VMEM((tt, D_HGRN), F32),
                        pltpu.VMEM((HGRN_UNROLL * HGRN_PAIRS, HGRN_LEVELS * c, PAIR), F32),
                        pltpu.VMEM((HGRN_UNROLL * HGRN_PAIRS, HGRN_LEVELS + 1, 2 * c, PAIR), BF16),
                        pltpu.VMEM((HGRN_UNROLL * HGRN_PAIRS, 2 * c, PAIR), BF16),
                        pltpu.VMEM((HGRN_HEADS, HEAD_DIM, HEAD_DIM), F32)],
        compiler_params=_params("parallel", "arbitrary"),
    )(proj, proj, proj, proj, lower.reshape(1, D_HGRN), norm_g.reshape(1, D_HGRN).astype(F32),
      jnp.asarray(mat, BF16), jnp.asarray(lvl))


def _ret_kernel(q_ref, k_ref, v_ref, g_ref, cos_ref, sin_ref, gg_ref, gb_ref, o_ref,
                intra_ref, qdec_ref, kdec_ref, st_ref, *, tt, c):
    log_gamma = [math.log1p(-(2.0 ** (-5.0 - h))) for h in range(RET_HEADS)]

    @pl.when(pl.program_id(1) == 0)
    def _():
        st_ref[...] = jnp.zeros(st_ref.shape, F32)
        row = lax.broadcasted_iota(jnp.int32, (c, c), 0)
        col = lax.broadcasted_iota(jnp.int32, (c, c), 1)
        diff = (row - col).astype(F32)
        rowl = lax.broadcasted_iota(jnp.int32, (c, HEAD_DIM), 0).astype(F32)
        for h in range(RET_HEADS):
            intra_ref[h] = jnp.where(diff >= 0, jnp.exp(log_gamma[h] * jnp.maximum(diff, 0.0)), 0.0)
            qdec_ref[h] = jnp.exp(log_gamma[h] * (rowl + 1.0))
            kdec_ref[h] = jnp.exp(log_gamma[h] * (c - 1.0 - rowl))

    k_scale = HEAD_DIM ** -0.5

    unroll = _chunks_per_trip(tt, c, RET_UNROLL)
    tasks = [(u, h) for u in range(unroll) for h in range(RET_HEADS)]
    cols = [slice(h * HEAD_DIM, (h + 1) * HEAD_DIM) for h in range(RET_HEADS)]

    def chunk(ci, carry):
        base = ci * (unroll * c)
        rows = [pl.ds(pl.multiple_of(base + u * c, c), c) for u in range(unroll)]
        cs = [cos_ref[rows[u], :] for u in range(unroll)]
        sn = [sin_ref[rows[u], :] for u in range(unroll)]
        qr, kr, vh, scores, outs = {}, {}, {}, {}, {}
        for t in tasks:
            u, h = t
            qh = q_ref[rows[u], cols[h]]
            kh = k_ref[rows[u], cols[h]]
            qr[t] = qh * cs[u] + pltpu.roll(qh, HEAD_DIM // 2, 1) * sn[u]
            kr[t] = (kh * cs[u] + pltpu.roll(kh, HEAD_DIM // 2, 1) * sn[u]) * k_scale
            vh[t] = v_ref[rows[u], cols[h]].astype(BF16)
            scores[t] = _dot_nt(qr[t].astype(BF16), kr[t].astype(BF16))
        for t in tasks:
            u, h = t
            st = st_ref[h]
            masked = (scores[t] * intra_ref[h]).astype(BF16)
            outs[t] = _dot(masked, vh[t]) + _dot((qr[t] * qdec_ref[h]).astype(BF16), st.astype(BF16))
            st_ref[h] = math.exp(log_gamma[h] * c) * st + _dot_tn((kr[t] * kdec_ref[h]).astype(BF16), vh[t])
        for t in tasks:
            u, h = t
            o = outs[t]
            mu = jnp.mean(o, axis=-1, keepdims=True)
            oc = o - mu
            var = jnp.mean(oc * oc, axis=-1, keepdims=True)
            on = oc * lax.rsqrt(var + LN_EPS) * gg_ref[:, cols[h]] + gb_ref[:, cols[h]]
            gh = g_ref[rows[u], cols[h]]
            o_ref[rows[u], cols[h]] = (gh * jax.nn.sigmoid(gh) * on).astype(o_ref.dtype)
        return carry

    lax.fori_loop(0, tt // (unroll * c), chunk, 0)


def _ret_mixer(proj, cos_t, sin_t, gn_g, gn_b, tt):
    b, t, _ = proj.shape
    c = min(RET_CHUNK, tt)
    blk = lambda n: pl.BlockSpec((None, tt, D_RET), lambda i, j: (i, j, RET_COL_BLOCK + n))
    tab = pl.BlockSpec((None, tt, HEAD_DIM), lambda i, j: (i, j, 0))
    vec = pl.BlockSpec((1, D_RET), lambda i, j: (0, 0))
    return pl.pallas_call(
        functools.partial(_ret_kernel, tt=tt, c=c),
        name="ret_mixer",
        grid=(b, t // tt),
        in_specs=[blk(0), blk(1), blk(2), blk(3), tab, tab, vec, vec],
        out_specs=pl.BlockSpec((None, tt, D_RET), lambda i, j: (i, j, 0)),
        out_shape=jax.ShapeDtypeStruct((b, t, D_RET), BF16),
        scratch_shapes=[pltpu.VMEM((RET_HEADS, c, c), F32),
                        pltpu.VMEM((RET_HEADS, c, HEAD_DIM), F32),
                        pltpu.VMEM((RET_HEADS, c, HEAD_DIM), F32),
                        pltpu.VMEM((RET_HEADS, HEAD_DIM, HEAD_DIM), F32)],
        compiler_params=_params("parallel", "arbitrary"),
    )(proj, proj, proj, proj, cos_t, sin_t, gn_g.reshape(1, D_RET).astype(F32), gn_b.reshape(1, D_RET).astype(F32))


def _outproj_kernel(yc_ref, yh_ref, yr_ref, w_ref, x_ref, g_ref, b_ref, o_ref, mix_ref, *, alpha):
    mix_ref[:, 0:D_CONV] = yc_ref[...]
    mix_ref[:, D_CONV:D_CONV + D_HGRN] = yh_ref[...]
    mix_ref[:, D_CONV + D_HGRN:] = yr_ref[...]
    z = alpha * x_ref[...] + _dot(mix_ref[...], w_ref[...])
    o_ref[...] = _layer_norm_rows(z, g_ref[...], b_ref[...])


def _outproj(yc, yh, yr, w_stack, layer, x2d, ln_g, ln_b, alpha, tm):
    m, d = x2d.shape
    d_mix = D_CONV + D_HGRN + D_RET
    vec = pl.BlockSpec((1, d), lambda i: (0, 0))
    return pl.pallas_call(
        functools.partial(_outproj_kernel, alpha=alpha),
        name="out_proj_ln",
        grid=(m // tm,),
        in_specs=[pl.BlockSpec((tm, D_CONV), lambda i: (i, 0)),
                  pl.BlockSpec((tm, D_HGRN), lambda i: (i, 0)),
                  pl.BlockSpec((tm, D_RET), lambda i: (i, 0)),
                  pl.BlockSpec((None, d_mix, d), lambda i: (layer, 0, 0)),
                  pl.BlockSpec((tm, d), lambda i: (i, 0)),
                  vec, vec],
        out_specs=pl.BlockSpec((tm, d), lambda i: (i, 0)),
        out_shape=jax.ShapeDtypeStruct((m, d), F32),
        scratch_shapes=[pltpu.VMEM((tm, d_mix), BF16)],
        compiler_params=_params("parallel"),
    )(yc, yh, yr, w_stack, x2d, ln_g.reshape(1, d).astype(F32), ln_b.reshape(1, d).astype(F32))


def _ffn_kernel(x_ref, w1_ref, w2_ref, g_ref, b_ref, o_ref, xb_ref, *, alpha):
    f = pl.program_id(1)

    @pl.when(f == 0)
    def _():
        xb_ref[...] = x_ref[...].astype(BF16)
        o_ref[...] = jnp.zeros(o_ref.shape, F32)

    hid = jnp.maximum(_dot(xb_ref[...], w1_ref[...].astype(BF16)), 0.0)
    o_ref[...] += _dot((hid * hid).astype(BF16), w2_ref[...].astype(BF16))

    @pl.when(f == pl.num_programs(1) - 1)
    def _():
        z = alpha * x_ref[...] + o_ref[...]
        o_ref[...] = _layer_norm_rows(z, g_ref[...], b_ref[...])


def _ffn(x2d, w1_stack, w2_stack, layer, ln_g, ln_b, alpha, tm, tf):
    m, d = x2d.shape
    d_ff = w1_stack.shape[-1]
    vec = pl.BlockSpec((1, d), lambda i, f: (0, 0))
    return pl.pallas_call(
        functools.partial(_ffn_kernel, alpha=alpha),
        name="ffn_ln",
        grid=(m // tm, d_ff // tf),
        in_specs=[pl.BlockSpec((tm, d), lambda i, f: (i, 0), pipeline_mode=pl.Buffered(1)),
                  pl.BlockSpec((None, d, tf), lambda i, f: (layer, 0, f)),
                  pl.BlockSpec((None, tf, d), lambda i, f: (layer, f, 0)),
                  vec, vec],
        out_specs=pl.BlockSpec((tm, d), lambda i, f: (i, 0), pipeline_mode=pl.Buffered(1)),
        out_shape=jax.ShapeDtypeStruct((m, d), F32),
        scratch_shapes=[pltpu.VMEM((tm, d), BF16)],
        compiler_params=_params("parallel", "arbitrary"),
    )(x2d, w1_stack, w2_stack, ln_g.reshape(1, d).astype(F32), ln_b.reshape(1, d).astype(F32))


def kernel(x, positions, w_in, w_dw, b_dw, conv_ln_g, conv_ln_b, hgrn_lb, hgrn_norm_g, ret_gn_g, ret_gn_b,
           w_out, ln1_g, ln1_b, w_ff1, w_ff2, ln2_g, ln2_b):
    b, t, d = x.shape
    depth = w_in.shape[0]
    alpha = (2.0 * depth) ** 0.25
    m = b * t
    tt = min(512, t)
    tm = min(1024, m)
    tm_out = min(512, m)
    tn = 2 * D_CONV
    tf = 512

    w_in_b = w_in.astype(BF16)
    w_out_b = w_out.astype(BF16)

    lower = _lower_bounds(hgrn_lb)
    cos_t, sin_t = _rope_tables(positions, tt)

    x2d = x.reshape(m, d).astype(F32)
    for l in range(depth):
        proj = _proj(x2d, w_in_b, l, tm, tn).reshape(b, t, -1)
        yh = _hgrn_mixer(proj, lower[l], hgrn_norm_g[l], tt)
        yr = _ret_mixer(proj, cos_t, sin_t, ret_gn_g[l], ret_gn_b[l], tt)
        yc = _conv_mixer(proj, w_dw[l], b_dw[l], conv_ln_g[l], conv_ln_b[l], tt)
        x2d = _outproj(yc.reshape(m, D_CONV), yh.reshape(m, D_HGRN), yr.reshape(m, D_RET),
                       w_out_b, l, x2d, ln1_g[l], ln1_b[l], alpha, tm_out)
        x2d = _ffn(x2d, w_ff1, w_ff2, l, ln2_g[l], ln2_b[l], alpha, tm, tf)
    return x2d.reshape(b, t, d).astype(x.dtype)
```

```python
import functools
import math

import numpy as np
import jax
import jax.numpy as jnp
from jax import lax
from jax.experimental import pallas as pl
from jax.experimental.pallas import tpu as pltpu

F32 = jnp.float32
BF16 = jnp.bfloat16

D_CONV = 512
D_HGRN = 768
D_RET = 768
CONV_WIDTH = 31
HEAD_DIM = 128
HGRN_HEADS = D_HGRN // HEAD_DIM
RET_HEADS = D_RET // HEAD_DIM
ROPE_BASE = 10000.0
LN_EPS = 1e-5

LANES = 128
SUBLANES = 8
HGRN_CHUNK = 64
HGRN_LEVELS = 6
HGRN_PAIRS = HGRN_HEADS // 2
HGRN_UNROLL = 8
PAIR = 2 * HEAD_DIM
RET_CHUNK = 128
RET_UNROLL = 1
CONV_HALO = 32
CONV_ROWS = 128
NORM_ROWS = 64
OUT_ROWS = 256

VMEM_LIMIT_BYTES = 56 * 1024 * 1024

HGRN_COL_BLOCK = 0
RET_COL_BLOCK = 4 * D_HGRN // D_RET
CONV_COL_BLOCK = (4 * D_HGRN + 4 * D_RET) // D_CONV


def _dot(a, b):
    return jnp.dot(a, b, preferred_element_type=F32)


def _dot_nt(a, b):
    return lax.dot_general(a, b, (((1,), (1,)), ((), ())), preferred_element_type=F32)


def _dot_tn(a, b):
    return lax.dot_general(a, b, (((0,), (0,)), ((), ())), preferred_element_type=F32)


def _sigmoid(x):
    return 0.5 * jnp.tanh(0.5 * x) + 0.5


def _chunks_per_trip(tt, chunk, want):
    return max(u for u in range(1, want + 1) if tt % (u * chunk) == 0)


def _params(*semantics):
    return pltpu.CompilerParams(dimension_semantics=semantics, vmem_limit_bytes=VMEM_LIMIT_BYTES)


def _layer_norm_rows(z, g, b):
    mu = jnp.mean(z, axis=-1, keepdims=True)
    zc = z - mu
    var = jnp.mean(zc * zc, axis=-1, keepdims=True)
    return zc * lax.rsqrt(var + LN_EPS) * g + b


def _lower_bounds_kernel(lb_ref, o_ref):
    depth = lb_ref.shape[0]
    rows = [lb_ref[l:l + 1, :] for l in range(depth)]
    m = functools.reduce(jnp.maximum, rows)
    ex = [jnp.exp(r - m) for r in rows]
    tot = functools.reduce(jnp.add, ex)
    p = [e / tot for e in ex]
    c = p[0]
    for l in range(depth):
        if l > 0:
            c = c + p[l]
        o_ref[l:l + 1, :] = jnp.clip(c - p[0], 0.0, 1.0 - 1e-6)


def _lower_bounds(hgrn_lb):
    return pl.pallas_call(
        _lower_bounds_kernel,
        name="lower_bounds",
        out_shape=jax.ShapeDtypeStruct(hgrn_lb.shape, F32),
    )(hgrn_lb.astype(F32))


def _rope_kernel(pos_ref, inv_ref, cos_ref, sin_ref):
    ang = pos_ref[...].astype(F32) * inv_ref[...]
    lane = lax.broadcasted_iota(jnp.int32, ang.shape, 1)
    s = jnp.sin(ang)
    cos_ref[...] = jnp.cos(ang)
    sin_ref[...] = jnp.where(lane < HEAD_DIM // 2, -s, s)


def _rope_tables(positions, tt):
    b, t = positions.shape
    half = HEAD_DIM // 2
    inv = ROPE_BASE ** (-jnp.arange(half, dtype=F32) / half)
    inv2 = jnp.concatenate([inv, inv]).reshape(1, HEAD_DIM)
    tab = jax.ShapeDtypeStruct((b, t, HEAD_DIM), F32)
    return pl.pallas_call(
        _rope_kernel,
        name="rope_tables",
        grid=(b, t // tt),
        in_specs=[pl.BlockSpec((None, tt, 1), lambda i, j: (i, j, 0)),
                  pl.BlockSpec((1, HEAD_DIM), lambda i, j: (0, 0))],
        out_specs=[pl.BlockSpec((None, tt, HEAD_DIM), lambda i, j: (i, j, 0))] * 2,
        out_shape=[tab, tab],
        compiler_params=_params("parallel", "parallel"),
    )(positions.reshape(b, t, 1), inv2)


def _proj_kernel(x_ref, w_ref, o_ref, xb_ref):
    @pl.when(pl.program_id(1) == 0)
    def _():
        xb_ref[...] = x_ref[...].astype(BF16)

    o_ref[...] = _dot(xb_ref[...], w_ref[...])


def _proj(x2d, w_stack, layer, tm, tn):
    m, k = x2d.shape
    n = w_stack.shape[-1]
    n_blocks = n // tn
    first = 2 * D_CONV // tn
    return pl.pallas_call(
        _proj_kernel,
        name="in_proj",
        grid=(m // tm, n_blocks),
        in_specs=[pl.BlockSpec((tm, k), lambda i, j: (i, 0)),
                  pl.BlockSpec((None, k, tn), lambda i, j: (layer, 0, (j + first) % n_blocks))],
        out_specs=pl.BlockSpec((tm, tn), lambda i, j: (i, j)),
        out_shape=jax.ShapeDtypeStruct((m, n), F32),
        scratch_shapes=[pltpu.VMEM((tm, k), BF16)],
        compiler_params=_params("parallel", "arbitrary"),
    )(x2d, w_stack)


def _conv_kernel(a_ref, g_ref, w_ref, b_ref, lng_ref, lnb_ref, o_ref, u_ref, y_ref, *, tt):
    @pl.when(pl.program_id(1) == 0)
    def _():
        u_ref[0:CONV_HALO, :] = jnp.zeros((CONV_HALO, D_CONV), F32)

    @pl.when(pl.program_id(1) > 0)
    def _():
        u_ref[0:CONV_HALO, :] = u_ref[tt:tt + CONV_HALO, :]

    u_ref[CONV_HALO:CONV_HALO + tt, :] = a_ref[...] * _sigmoid(g_ref[...])

    first_tap = CONV_HALO - (CONV_WIDTH - 1)
    rows = min(CONV_ROWS, tt)
    win = rows + CONV_HALO
    for r0 in range(0, tt, rows):
        for c0 in range(0, D_CONV, LANES):
            acc = jnp.broadcast_to(b_ref[:, c0:c0 + LANES], (rows, LANES))
            for r in range(SUBLANES):
                window = u_ref[r0:r0 + win, c0:c0 + LANES]
                shifted = window if r == 0 else pltpu.roll(window, win - r, 0)
                for a in range(win // SUBLANES):
                    j = SUBLANES * a + r - first_tap
                    if 0 <= j < CONV_WIDTH:
                        acc = acc + w_ref[j:j + 1, c0:c0 + LANES] * shifted[SUBLANES * a:SUBLANES * a + rows]
            y_ref[r0:r0 + rows, c0:c0 + LANES] = acc

    nrows = min(NORM_ROWS, tt)
    for r0 in range(0, tt, nrows):
        v = _layer_norm_rows(y_ref[r0:r0 + nrows, :], lng_ref[...], lnb_ref[...])
        o_ref[r0:r0 + nrows, :] = (v * _sigmoid(v)).astype(o_ref.dtype)


def _conv_mixer(proj, w_dw, b_dw, ln_g, ln_b, tt):
    b, t, _ = proj.shape
    row = lambda v: v.reshape(1, D_CONV).astype(F32)
    vec = pl.BlockSpec((1, D_CONV), lambda i, j: (0, 0))
    blk = lambda n: pl.BlockSpec((None, tt, D_CONV), lambda i, j: (i, j, CONV_COL_BLOCK + n))
    return pl.pallas_call(
        functools.partial(_conv_kernel, tt=tt),
        name="conv_mixer",
        grid=(b, t // tt),
        in_specs=[blk(0), blk(1),
                  pl.BlockSpec((CONV_WIDTH, D_CONV), lambda i, j: (0, 0)),
                  vec, vec, vec],
        out_specs=pl.BlockSpec((None, tt, D_CONV), lambda i, j: (i, j, 0)),
        out_shape=jax.ShapeDtypeStruct((b, t, D_CONV), BF16),
        scratch_shapes=[pltpu.VMEM((CONV_HALO + tt, D_CONV), F32),
                        pltpu.VMEM((tt, D_CONV), F32)],
        compiler_params=_params("parallel", "arbitrary"),
    )(proj, proj, w_dw.astype(F32), row(b_dw), row(ln_g), row(ln_b))


def _hgrn_tables():
    c = HGRN_CHUNK
    mat = np.zeros((HGRN_LEVELS, c, c), np.float32)
    for l in range(1, HGRN_LEVELS):
        half = 1 << l
        for t in range(c):
            mid = ((t >> (l + 1)) << (l + 1)) + half - 1
            if t > mid:
                mat[l - 1, t, mid + 1:t + 1] = 1.0
            else:
                mat[l - 1, t, t + 1:mid + 1] = 1.0
    mat[HGRN_LEVELS - 1] = np.tril(np.ones((c, c), np.float32))
    mat = mat.reshape(-1, c)
    t = np.arange(c)[:, None]
    s = np.arange(c)[None, :]
    x = t ^ s
    lvl = np.zeros((c, c), np.int32)
    for l in range(1, HGRN_LEVELS):
        lvl += (x >= (1 << l)).astype(np.int32)
    lvl = np.where(t == s, HGRN_LEVELS, lvl)
    lvl = np.where(t < s, HGRN_LEVELS + 1, lvl)
    return np.concatenate([mat, mat, mat], axis=1), np.concatenate([lvl, lvl], axis=1).astype(np.int32)


def _hgrn_kernel(q_ref, f_ref, i_ref, g_ref, lb_ref, ng_ref, mat_ref, lvl_ref, o_ref,
                 qs_ref, ks_ref, lf_ref, e_ref, kb_ref, ib_ref, st_ref, *, tt):
    c = HGRN_CHUNK

    @pl.when(pl.program_id(1) == 0)
    def _():
        st_ref[...] = jnp.zeros(st_ref.shape, F32)
        kb_ref[...] = jnp.zeros(kb_ref.shape, BF16)
        ib_ref[...] = jnp.zeros(ib_ref.shape, BF16)

    q = q_ref[...]
    qs_ref[...] = q * _sigmoid(q)
    z = f_ref[...]
    ez = jnp.exp(-jnp.abs(z))
    inv = 1.0 / (1.0 + ez)
    log_sig = jnp.minimum(z, 0.0) - jnp.log(1.0 + ez)
    lb = lb_ref[...]
    f_pos = lb + (1.0 - lb) * (jnp.where(z >= 0, 1.0, ez) * inv)
    lf_ref[...] = jnp.where(lb > 0, jnp.log(f_pos), log_sig)
    ks_ref[...] = (1.0 - lb) * (jnp.where(z >= 0, ez, 1.0) * inv)

    lvl = lvl_ref[...]
    odd_row = (lax.broadcasted_iota(jnp.int32, (c, PAIR), 0) & 1) == 1

    unroll = _chunks_per_trip(tt, c, HGRN_UNROLL)
    tasks = [(u, p) for u in range(unroll) for p in range(HGRN_PAIRS)]

    def chunk(ci, carry):
        base = ci * (unroll * c)
        rows = [pl.ds(pl.multiple_of(base + u * c, c), c) for u in range(unroll)]
        lanes = [slice(p * PAIR, (p + 1) * PAIR) for p in range(HGRN_PAIRS)]
        slot = {t: n for n, t in enumerate(tasks)}

        lf, sums = {}, {}
        for t in tasks:
            u, p = t
            lf[t] = lf_ref[rows[u], lanes[p]]
            hi = lf[t].astype(BF16)
            r1 = lf[t] - hi.astype(F32)
            mid = r1.astype(BF16)
            lo = (r1 - mid.astype(F32)).astype(BF16)
            sums[t] = _dot(mat_ref[...], jnp.concatenate([hi, mid, lo], axis=0))

        q2, k2, i2, erem = {}, {}, {}, {}
        for t in tasks:
            u, p = t
            e_ref[slot[t]] = jnp.exp(sums[t])
            cum = sums[t][(HGRN_LEVELS - 1) * c:HGRN_LEVELS * c]
            erem[t] = jnp.exp(cum[c - 1:c, :] - cum)
            q2[t] = qs_ref[rows[u], lanes[p]]
            k2[t] = ks_ref[rows[u], lanes[p]]
            i2[t] = i_ref[rows[u], lanes[p]].astype(BF16)
            ib_ref[slot[t], 0:c, 0:HEAD_DIM] = i2[t][:, 0:HEAD_DIM]
            ib_ref[slot[t], c:2 * c, HEAD_DIM:PAIR] = i2[t][:, HEAD_DIM:PAIR]

        def scores(t, level, el):
            kt = (k2[t] * el).astype(BF16) if el is not None else k2[t].astype(BF16)
            qt = (q2[t] * el).astype(BF16) if el is not None else q2[t].astype(BF16)
            kb_ref[slot[t], level, 0:c, 0:HEAD_DIM] = kt[:, 0:HEAD_DIM]
            kb_ref[slot[t], level, c:2 * c, HEAD_DIM:PAIR] = kt[:, HEAD_DIM:PAIR]
            return _dot_nt(qt, kb_ref[slot[t], level])

        att = {t: jnp.where(lvl == HGRN_LEVELS, scores(t, HGRN_LEVELS, None), 0.0) for t in tasks}
        for t in tasks:
            att[t] = jnp.where(lvl == 0, scores(t, 0, jnp.exp(jnp.where(odd_row, lf[t], 0.0))), att[t])
        for l in range(1, HGRN_LEVELS):
            for t in tasks:
                att[t] = jnp.where(lvl == l, scores(t, l, e_ref[slot[t], (l - 1) * c:l * c, :]), att[t])

        o2 = {t: _dot(att[t].astype(BF16), ib_ref[slot[t]]) for t in tasks}

        for t in tasks:
            u, p = t
            ecum = e_ref[slot[t], (HGRN_LEVELS - 1) * c:HGRN_LEVELS * c, :]
            glast = e_ref[slot[t], HGRN_LEVELS * c - 1:HGRN_LEVELS * c, :]
            qc = (q2[t] * ecum).astype(BF16)
            kr = (k2[t] * erem[t]).astype(BF16)
            gate = ng_ref[:, lanes[p]] * _sigmoid(g_ref[rows[u], lanes[p]])
            for hh in range(2):
                h = 2 * p + hh
                hl = slice(hh * HEAD_DIM, (hh + 1) * HEAD_DIM)
                st = st_ref[h]
                o = o2[t][:, hl] + _dot_nt(qc[:, hl], st.astype(BF16))
                st_ref[h] = st * glast[:, hl] + _dot_tn(i2[t][:, hl], kr[:, hl])
                o = o * lax.rsqrt(jnp.mean(o * o, axis=-1, keepdims=True) + LN_EPS)
                o_ref[rows[u], h * HEAD_DIM:(h + 1) * HEAD_DIM] = (o * gate[:, hl]).astype(o_ref.dtype)
        return carry

    lax.fori_loop(0, tt // (unroll * c), chunk, 0)


def _hgrn_mixer(proj, lower, norm_g, tt):
    b, t, _ = proj.shape
    mat, lvl = _hgrn_tables()
    c = HGRN_CHUNK
    blk = lambda n: pl.BlockSpec((None, tt, D_HGRN), lambda i, j: (i, j, HGRN_COL_BLOCK + n))
    vec = pl.BlockSpec((1, D_HGRN), lambda i, j: (0, 0))
    return pl.pallas_call(
        functools.partial(_hgrn_kernel, tt=tt),
        name="hgrn_mixer",
        grid=(b, t // tt),
        in_specs=[blk(0), blk(1), blk(2), blk(3), vec, vec,
                  pl.BlockSpec(mat.shape, lambda i, j: (0, 0)),
                  pl.BlockSpec(lvl.shape, lambda i, j: (0, 0))],
        out_specs=pl.BlockSpec((None, tt, D_HGRN), lambda i, j: (i, j, 0)),
        out_shape=jax.ShapeDtypeStruct((b, t, D_HGRN), BF16),
        scratch_shapes=[pltpu.VMEM((tt, D_HGRN), F32),
                        pltpu.VMEM((tt, D_HGRN), F32),
                        pltpu.VMEM((tt, D_HGRN), F32),
                        pltpu.VMEM((HGRN_UNROLL * HGRN_PAIRS, HGRN_LEVELS * c, PAIR), F32),
                        pltpu.VMEM((HGRN_UNROLL * HGRN_PAIRS, HGRN_LEVELS + 1, 2 * c, PAIR), BF16),
                        pltpu.VMEM((HGRN_UNROLL * HGRN_PAIRS, 2 * c, PAIR), BF16),
                        pltpu.VMEM((HGRN_HEADS, HEAD_DIM, HEAD_DIM), F32)],
        compiler_params=_params("parallel", "arbitrary"),
    )(proj, proj, proj, proj, lower.reshape(1, D_HGRN), norm_g.reshape(1, D_HGRN).astype(F32),
      jnp.asarray(mat, BF16), jnp.asarray(lvl))


def _ret_kernel(q_ref, k_ref, v_ref, g_ref, cos_ref, sin_ref, gg_ref, gb_ref, o_ref,
                intra_ref, qdec_ref, kdec_ref, st_ref, *, tt, c):
    log_gamma = [math.log1p(-(2.0 ** (-5.0 - h))) for h in range(RET_HEADS)]

    @pl.when(pl.program_id(1) == 0)
    def _():
        st_ref[...] = jnp.zeros(st_ref.shape, F32)
        row = lax.broadcasted_iota(jnp.int32, (c, c), 0)
        col = lax.broadcasted_iota(jnp.int32, (c, c), 1)
        diff = (row - col).astype(F32)
        rowl = lax.broadcasted_iota(jnp.int32, (c, HEAD_DIM), 0).astype(F32)
        for h in range(RET_HEADS):
            intra_ref[h] = jnp.where(diff >= 0, jnp.exp(log_gamma[h] * jnp.maximum(diff, 0.0)), 0.0)
            qdec_ref[h] = jnp.exp(log_gamma[h] * (rowl + 1.0))
            kdec_ref[h] = jnp.exp(log_gamma[h] * (c - 1.0 - rowl))

    k_scale = HEAD_DIM ** -0.5

    unroll = _chunks_per_trip(tt, c, RET_UNROLL)
    tasks = [(u, h) for u in range(unroll) for h in range(RET_HEADS)]
    cols = [slice(h * HEAD_DIM, (h + 1) * HEAD_DIM) for h in range(RET_HEADS)]

    def chunk(ci, carry):
        base = ci * (unroll * c)
        rows = [pl.ds(pl.multiple_of(base + u * c, c), c) for u in range(unroll)]
        cs = [cos_ref[rows[u], :] for u in range(unroll)]
        sn = [sin_ref[rows[u], :] for u in range(unroll)]
        qr, kr, vh, scores, outs = {}, {}, {}, {}, {}
        for t in tasks:
            u, h = t
            qh = q_ref[rows[u], cols[h]]
            kh = k_ref[rows[u], cols[h]]
            qr[t] = qh * cs[u] + pltpu.roll(qh, HEAD_DIM // 2, 1) * sn[u]
            kr[t] = (kh * cs[u] + pltpu.roll(kh, HEAD_DIM // 2, 1) * sn[u]) * k_scale
            vh[t] = v_ref[rows[u], cols[h]].astype(BF16)
            scores[t] = _dot_nt(qr[t].astype(BF16), kr[t].astype(BF16))
        for t in tasks:
            u, h = t
            st = st_ref[h]
            masked = (scores[t] * intra_ref[h]).astype(BF16)
            outs[t] = _dot(masked, vh[t]) + _dot((qr[t] * qdec_ref[h]).astype(BF16), st.astype(BF16))
            st_ref[h] = math.exp(log_gamma[h] * c) * st + _dot_tn((kr[t] * kdec_ref[h]).astype(BF16), vh[t])
        for t in tasks:
            u, h = t
            o = outs[t]
            mu = jnp.mean(o, axis=-1, keepdims=True)
            oc = o - mu
            var = jnp.mean(oc * oc, axis=-1, keepdims=True)
            on = oc * lax.rsqrt(var + LN_EPS) * gg_ref[:, cols[h]] + gb_ref[:, cols[h]]
            gh = g_ref[rows[u], cols[h]]
            o_ref[rows[u], cols[h]] = (gh * _sigmoid(gh) * on).astype(o_ref.dtype)
        return carry

    lax.fori_loop(0, tt // (unroll * c), chunk, 0)


def _ret_mixer(proj, cos_t, sin_t, gn_g, gn_b, tt):
    b, t, _ = proj.shape
    c = min(RET_CHUNK, tt)
    blk = lambda n: pl.BlockSpec((None, tt, D_RET), lambda i, j: (i, j, RET_COL_BLOCK + n))
    tab = pl.BlockSpec((None, tt, HEAD_DIM), lambda i, j: (i, j, 0))
    vec = pl.BlockSpec((1, D_RET), lambda i, j: (0, 0))
    return pl.pallas_call(
        functools.partial(_ret_kernel, tt=tt, c=c),
        name="ret_mixer",
        grid=(b, t // tt),
        in_specs=[blk(0), blk(1), blk(2), blk(3), tab, tab, vec, vec],
        out_specs=pl.BlockSpec((None, tt, D_RET), lambda i, j: (i, j, 0)),
        out_shape=jax.ShapeDtypeStruct((b, t, D_RET), BF16),
        scratch_shapes=[pltpu.VMEM((RET_HEADS, c, c), F32),
                        pltpu.VMEM((RET_HEADS, c, HEAD_DIM), F32),
                        pltpu.VMEM((RET_HEADS, c, HEAD_DIM), F32),
                        pltpu.VMEM((RET_HEADS, HEAD_DIM, HEAD_DIM), F32)],
        compiler_params=_params("parallel", "arbitrary"),
    )(proj, proj, proj, proj, cos_t, sin_t, gn_g.reshape(1, D_RET).astype(F32), gn_b.reshape(1, D_RET).astype(F32))


def _outproj_kernel(yc_ref, yh_ref, yr_ref, w_ref, x_ref, g_ref, b_ref, o_ref, mix_ref, *, alpha):
    mix_ref[:, 0:D_CONV] = yc_ref[...]
    mix_ref[:, D_CONV:D_CONV + D_HGRN] = yh_ref[...]
    mix_ref[:, D_CONV + D_HGRN:] = yr_ref[...]
    tm = o_ref.shape[0]
    rows = min(OUT_ROWS, tm)
    for r0 in range(0, tm, rows):
        z = alpha * x_ref[r0:r0 + rows, :] + _dot(mix_ref[r0:r0 + rows, :], w_ref[...])
        o_ref[r0:r0 + rows, :] = _layer_norm_rows(z, g_ref[...], b_ref[...])


def _outproj(yc, yh, yr, w_stack, layer, x2d, ln_g, ln_b, alpha, tm):
    m, d = x2d.shape
    d_mix = D_CONV + D_HGRN + D_RET
    vec = pl.BlockSpec((1, d), lambda i: (0, 0))
    return pl.pallas_call(
        functools.partial(_outproj_kernel, alpha=alpha),
        name="out_proj_ln",
        grid=(m // tm,),
        in_specs=[pl.BlockSpec((tm, D_CONV), lambda i: (i, 0)),
                  pl.BlockSpec((tm, D_HGRN), lambda i: (i, 0)),
                  pl.BlockSpec((tm, D_RET), lambda i: (i, 0)),
                  pl.BlockSpec((None, d_mix, d), lambda i: (layer, 0, 0)),
                  pl.BlockSpec((tm, d), lambda i: (i, 0)),
                  vec, vec],
        out_specs=pl.BlockSpec((tm, d), lambda i: (i, 0)),
        out_shape=jax.ShapeDtypeStruct((m, d), F32),
        scratch_shapes=[pltpu.VMEM((tm, d_mix), BF16)],
        compiler_params=_params("parallel"),
    )(yc, yh, yr, w_stack, x2d, ln_g.reshape(1, d).astype(F32), ln_b.reshape(1, d).astype(F32))


def _ffn_kernel(x_ref, w1_ref, w2_ref, g_ref, b_ref, o_ref, xb_ref, *, alpha):
    f = pl.program_id(1)

    @pl.when(f == 0)
    def _():
        xb_ref[...] = x_ref[...].astype(BF16)
        o_ref[...] = jnp.zeros(o_ref.shape, F32)

    last = pl.num_programs(1) - 1

    def hidden():
        hid = jnp.maximum(_dot(xb_ref[...], w1_ref[...].astype(BF16)), 0.0)
        return (hid * hid).astype(BF16)

    @pl.when(f < last)
    def _():
        o_ref[...] += _dot(hidden(), w2_ref[...].astype(BF16))

    @pl.when(f == last)
    def _():
        hid = hidden()
        w2 = w2_ref[...].astype(BF16)
        tm = o_ref.shape[0]
        rows = min(OUT_ROWS, tm)
        for r0 in range(0, tm, rows):
            z = alpha * x_ref[r0:r0 + rows, :] + (o_ref[r0:r0 + rows, :] + _dot(hid[r0:r0 + rows, :], w2))
            o_ref[r0:r0 + rows, :] = _layer_norm_rows(z, g_ref[...], b_ref[...])


def _ffn(x2d, w1_stack, w2_stack, layer, ln_g, ln_b, alpha, tm, tf):
    m, d = x2d.shape
    d_ff = w1_stack.shape[-1]
    vec = pl.BlockSpec((1, d), lambda i, f: (0, 0))
    return pl.pallas_call(
        functools.partial(_ffn_kernel, alpha=alpha),
        name="ffn_ln",
        grid=(m // tm, d_ff // tf),
        in_specs=[pl.BlockSpec((tm, d), lambda i, f: (i, 0), pipeline_mode=pl.Buffered(1)),
                  pl.BlockSpec((None, d, tf), lambda i, f: (layer, 0, f)),
                  pl.BlockSpec((None, tf, d), lambda i, f: (layer, f, 0)),
                  vec, vec],
        out_specs=pl.BlockSpec((tm, d), lambda i, f: (i, 0), pipeline_mode=pl.Buffered(1)),
        out_shape=jax.ShapeDtypeStruct((m, d), F32),
        scratch_shapes=[pltpu.VMEM((tm, d), BF16)],
        compiler_params=_params("parallel", "arbitrary"),
    )(x2d, w1_stack, w2_stack, ln_g.reshape(1, d).astype(F32), ln_b.reshape(1, d).astype(F32))


def kernel(x, positions, w_in, w_dw, b_dw, conv_ln_g, conv_ln_b, hgrn_lb, hgrn_norm_g, ret_gn_g, ret_gn_b,
           w_out, ln1_g, ln1_b, w_ff1, w_ff2, ln2_g, ln2_b):
    b, t, d = x.shape
    depth = w_in.shape[0]
    alpha = (2.0 * depth) ** 0.25
    m = b * t
    tt = min(512, t)
    tm = min(1024, m)
    tm_out = min(512, m)
    tn = 2 * D_CONV
    tf = 512

    w_in_b = w_in.astype(BF16)
    w_out_b = w_out.astype(BF16)

    lower = _lower_bounds(hgrn_lb)
    cos_t, sin_t = _rope_tables(positions, tt)

    x2d = x.reshape(m, d).astype(F32)
    for l in range(depth):
        proj = _proj(x2d, w_in_b, l, tm, tn).reshape(b, t, -1)
        yh = _hgrn_mixer(proj, lower[l], hgrn_norm_g[l], tt)
        yr = _ret_mixer(proj, cos_t, sin_t, ret_gn_g[l], ret_gn_b[l], tt)
        yc = _conv_mixer(proj, w_dw[l], b_dw[l], conv_ln_g[l], conv_ln_b[l], tt)
        x2d = _outproj(yc.reshape(m, D_CONV), yh.reshape(m, D_HGRN), yr.reshape(m, D_RET),
                       w_out_b, l, x2d, ln1_g[l], ln1_b[l], alpha, tm_out)
        x2d = _ffn(x2d, w_ff1, w_ff2, l, ln2_g[l], ln2_b[l], alpha, tm, tf)
    return x2d.reshape(b, t, d).astype(x.dtype)
```

```python
import functools
import math

import numpy as np
import jax
import jax.numpy as jnp
from jax import lax
from jax.experimental import pallas as pl
from jax.experimental.pallas import tpu as pltpu

F32 = jnp.float32
BF16 = jnp.bfloat16

D_CONV = 512
D_HGRN = 768
D_RET = 768
CONV_WIDTH = 31
HEAD_DIM = 128
HGRN_HEADS = D_HGRN // HEAD_DIM
RET_HEADS = D_RET // HEAD_DIM
ROPE_BASE = 10000.0
LN_EPS = 1e-5

LANES = 128
SUBLANES = 8
HGRN_CHUNK = 64
HGRN_LEVELS = 6
HGRN_PAIRS = HGRN_HEADS // 2
HGRN_UNROLL = 8
PAIR = 2 * HEAD_DIM
RET_CHUNK = 128
RET_UNROLL = 1
CONV_HALO = 32
CONV_ROWS = 128
NORM_ROWS = 64
FFN_SLOTS = 3
OUT_ROWS = 256

VMEM_LIMIT_BYTES = 56 * 1024 * 1024

HGRN_COL_BLOCK = 0
RET_COL_BLOCK = 4 * D_HGRN // D_RET
CONV_COL_BLOCK = (4 * D_HGRN + 4 * D_RET) // D_CONV


def _dot(a, b):
    return jnp.dot(a, b, preferred_element_type=F32)


def _dot_nt(a, b):
    return lax.dot_general(a, b, (((1,), (1,)), ((), ())), preferred_element_type=F32)


def _dot_tn(a, b):
    return lax.dot_general(a, b, (((0,), (0,)), ((), ())), preferred_element_type=F32)


def _sigmoid(x):
    return 0.5 * jnp.tanh(0.5 * x) + 0.5


def _chunks_per_trip(tt, chunk, want):
    return max(u for u in range(1, want + 1) if tt % (u * chunk) == 0)


def _params(*semantics):
    return pltpu.CompilerParams(dimension_semantics=semantics, vmem_limit_bytes=VMEM_LIMIT_BYTES)


def _layer_norm_rows(z, g, b):
    mu = jnp.mean(z, axis=-1, keepdims=True)
    zc = z - mu
    var = jnp.mean(zc * zc, axis=-1, keepdims=True)
    return zc * lax.rsqrt(var + LN_EPS) * g + b


def _lower_bounds_kernel(lb_ref, o_ref):
    depth = lb_ref.shape[0]
    rows = [lb_ref[l:l + 1, :] for l in range(depth)]
    m = functools.reduce(jnp.maximum, rows)
    ex = [jnp.exp(r - m) for r in rows]
    tot = functools.reduce(jnp.add, ex)
    p = [e / tot for e in ex]
    c = p[0]
    for l in range(depth):
        if l > 0:
            c = c + p[l]
        o_ref[l:l + 1, :] = jnp.clip(c - p[0], 0.0, 1.0 - 1e-6)


def _lower_bounds(hgrn_lb):
    return pl.pallas_call(
        _lower_bounds_kernel,
        name="lower_bounds",
        out_shape=jax.ShapeDtypeStruct(hgrn_lb.shape, F32),
    )(hgrn_lb.astype(F32))


def _rope_kernel(pos_ref, inv_ref, cos_ref, sin_ref):
    ang = pos_ref[...].astype(F32) * inv_ref[...]
    lane = lax.broadcasted_iota(jnp.int32, ang.shape, 1)
    s = jnp.sin(ang)
    cos_ref[...] = jnp.cos(ang)
    sin_ref[...] = jnp.where(lane < HEAD_DIM // 2, -s, s)


def _rope_tables(positions, tt):
    b, t = positions.shape
    half = HEAD_DIM // 2
    inv = ROPE_BASE ** (-jnp.arange(half, dtype=F32) / half)
    inv2 = jnp.concatenate([inv, inv]).reshape(1, HEAD_DIM)
    tab = jax.ShapeDtypeStruct((b, t, HEAD_DIM), F32)
    return pl.pallas_call(
        _rope_kernel,
        name="rope_tables",
        grid=(b, t // tt),
        in_specs=[pl.BlockSpec((None, tt, 1), lambda i, j: (i, j, 0)),
                  pl.BlockSpec((1, HEAD_DIM), lambda i, j: (0, 0))],
        out_specs=[pl.BlockSpec((None, tt, HEAD_DIM), lambda i, j: (i, j, 0))] * 2,
        out_shape=[tab, tab],
        compiler_params=_params("parallel", "parallel"),
    )(positions.reshape(b, t, 1), inv2)


def _proj_kernel(x_ref, w_ref, o_ref, xb_ref):
    @pl.when(pl.program_id(1) == 0)
    def _():
        xb_ref[...] = x_ref[...].astype(BF16)

    o_ref[...] = _dot(xb_ref[...], w_ref[...])


def _proj(x2d, w_stack, layer, tm, tn):
    m, k = x2d.shape
    n = w_stack.shape[-1]
    n_blocks = n // tn
    first = 2 * D_CONV // tn
    return pl.pallas_call(
        _proj_kernel,
        name="in_proj",
        grid=(m // tm, n_blocks),
        in_specs=[pl.BlockSpec((tm, k), lambda i, j: (i, 0)),
                  pl.BlockSpec((None, k, tn), lambda i, j: (layer, 0, (j + first) % n_blocks))],
        out_specs=pl.BlockSpec((tm, tn), lambda i, j: (i, j)),
        out_shape=jax.ShapeDtypeStruct((m, n), F32),
        scratch_shapes=[pltpu.VMEM((tm, k), BF16)],
        compiler_params=_params("parallel", "arbitrary"),
    )(x2d, w_stack)


def _conv_kernel(a_ref, g_ref, w_ref, b_ref, lng_ref, lnb_ref, o_ref, u_ref, y_ref, *, tt):
    @pl.when(pl.program_id(1) == 0)
    def _():
        u_ref[0:CONV_HALO, :] = jnp.zeros((CONV_HALO, D_CONV), F32)

    @pl.when(pl.program_id(1) > 0)
    def _():
        u_ref[0:CONV_HALO, :] = u_ref[tt:tt + CONV_HALO, :]

    u_ref[CONV_HALO:CONV_HALO + tt, :] = a_ref[...] * _sigmoid(g_ref[...])

    first_tap = CONV_HALO - (CONV_WIDTH - 1)
    rows = min(CONV_ROWS, tt)
    win = rows + CONV_HALO
    for r0 in range(0, tt, rows):
        for c0 in range(0, D_CONV, LANES):
            acc = jnp.broadcast_to(b_ref[:, c0:c0 + LANES], (rows, LANES))
            for r in range(SUBLANES):
                window = u_ref[r0:r0 + win, c0:c0 + LANES]
                shifted = window if r == 0 else pltpu.roll(window, win - r, 0)
                for a in range(win // SUBLANES):
                    j = SUBLANES * a + r - first_tap
                    if 0 <= j < CONV_WIDTH:
                        acc = acc + w_ref[j:j + 1, c0:c0 + LANES] * shifted[SUBLANES * a:SUBLANES * a + rows]
            y_ref[r0:r0 + rows, c0:c0 + LANES] = acc

    nrows = min(NORM_ROWS, tt)
    for r0 in range(0, tt, nrows):
        v = _layer_norm_rows(y_ref[r0:r0 + nrows, :], lng_ref[...], lnb_ref[...])
        o_ref[r0:r0 + nrows, :] = (v * _sigmoid(v)).astype(o_ref.dtype)


def _conv_mixer(proj, w_dw, b_dw, ln_g, ln_b, tt):
    b, t, _ = proj.shape
    row = lambda v: v.reshape(1, D_CONV).astype(F32)
    vec = pl.BlockSpec((1, D_CONV), lambda i, j: (0, 0))
    blk = lambda n: pl.BlockSpec((None, tt, D_CONV), lambda i, j: (i, j, CONV_COL_BLOCK + n))
    return pl.pallas_call(
        functools.partial(_conv_kernel, tt=tt),
        name="conv_mixer",
        grid=(b, t // tt),
        in_specs=[blk(0), blk(1),
                  pl.BlockSpec((CONV_WIDTH, D_CONV), lambda i, j: (0, 0)),
                  vec, vec, vec],
        out_specs=pl.BlockSpec((None, tt, D_CONV), lambda i, j: (i, j, 0)),
        out_shape=jax.ShapeDtypeStruct((b, t, D_CONV), BF16),
        scratch_shapes=[pltpu.VMEM((CONV_HALO + tt, D_CONV), F32),
                        pltpu.VMEM((tt, D_CONV), F32)],
        compiler_params=_params("parallel", "arbitrary"),
    )(proj, proj, w_dw.astype(F32), row(b_dw), row(ln_g), row(ln_b))


def _hgrn_tables():
    c = HGRN_CHUNK
    mat = np.zeros((HGRN_LEVELS, c, c), np.float32)
    for l in range(1, HGRN_LEVELS):
        half = 1 << l
        for t in range(c):
            mid = ((t >> (l + 1)) << (l + 1)) + half - 1
            if t > mid:
                mat[l - 1, t, mid + 1:t + 1] = 1.0
            else:
                mat[l - 1, t, t + 1:mid + 1] = 1.0
    mat[HGRN_LEVELS - 1] = np.tril(np.ones((c, c), np.float32))
    mat = mat.reshape(-1, c)
    t = np.arange(c)[:, None]
    s = np.arange(c)[None, :]
    x = t ^ s
    lvl = np.zeros((c, c), np.int32)
    for l in range(1, HGRN_LEVELS):
        lvl += (x >= (1 << l)).astype(np.int32)
    lvl = np.where(t == s, HGRN_LEVELS, lvl)
    lvl = np.where(t < s, HGRN_LEVELS + 1, lvl)
    return np.concatenate([mat, mat, mat], axis=1), np.concatenate([lvl, lvl], axis=1).astype(np.int32)


def _hgrn_kernel(q_ref, f_ref, i_ref, g_ref, lb_ref, ng_ref, mat_ref, lvl_ref, o_ref,
                 qs_ref, ks_ref, lf_ref, e_ref, kb_ref, ib_ref, st_ref, *, tt):
    c = HGRN_CHUNK

    @pl.when(pl.program_id(1) == 0)
    def _():
        st_ref[...] = jnp.zeros(st_ref.shape, F32)
        kb_ref[...] = jnp.zeros(kb_ref.shape, BF16)
        ib_ref[...] = jnp.zeros(ib_ref.shape, BF16)

    q = q_ref[...]
    qs_ref[...] = q * _sigmoid(q)
    z = f_ref[...]
    ez = jnp.exp(-jnp.abs(z))
    inv = 1.0 / (1.0 + ez)
    log_sig = jnp.minimum(z, 0.0) - jnp.log(1.0 + ez)
    lb = lb_ref[...]
    f_pos = lb + (1.0 - lb) * (jnp.where(z >= 0, 1.0, ez) * inv)
    lf_ref[...] = jnp.where(lb > 0, jnp.log(f_pos), log_sig)
    ks_ref[...] = (1.0 - lb) * (jnp.where(z >= 0, ez, 1.0) * inv)

    lvl = lvl_ref[...]
    odd_row = (lax.broadcasted_iota(jnp.int32, (c, PAIR), 0) & 1) == 1

    unroll = _chunks_per_trip(tt, c, HGRN_UNROLL)
    tasks = [(u, p) for u in range(unroll) for p in range(HGRN_PAIRS)]

    def chunk(ci, carry):
        base = ci * (unroll * c)
        rows = [pl.ds(pl.multiple_of(base + u * c, c), c) for u in range(unroll)]
        lanes = [slice(p * PAIR, (p + 1) * PAIR) for p in range(HGRN_PAIRS)]
        slot = {t: n for n, t in enumerate(tasks)}

        lf, sums = {}, {}
        for t in tasks:
            u, p = t
            lf[t] = lf_ref[rows[u], lanes[p]]
            hi = lf[t].astype(BF16)
            r1 = lf[t] - hi.astype(F32)
            mid = r1.astype(BF16)
            lo = (r1 - mid.astype(F32)).astype(BF16)
            sums[t] = _dot(mat_ref[...], jnp.concatenate([hi, mid, lo], axis=0))

        q2, k2, i2, erem = {}, {}, {}, {}
        for t in tasks:
            u, p = t
            e_ref[slot[t]] = jnp.exp(sums[t])
            cum = sums[t][(HGRN_LEVELS - 1) * c:HGRN_LEVELS * c]
            erem[t] = jnp.exp(cum[c - 1:c, :] - cum)
            q2[t] = qs_ref[rows[u], lanes[p]]
            k2[t] = ks_ref[rows[u], lanes[p]]
            i2[t] = i_ref[rows[u], lanes[p]].astype(BF16)
            ib_ref[slot[t], 0:c, 0:HEAD_DIM] = i2[t][:, 0:HEAD_DIM]
            ib_ref[slot[t], c:2 * c, HEAD_DIM:PAIR] = i2[t][:, HEAD_DIM:PAIR]

        def scores(t, level, el):
            kt = (k2[t] * el).astype(BF16) if el is not None else k2[t].astype(BF16)
            qt = (q2[t] * el).astype(BF16) if el is not None else q2[t].astype(BF16)
            kb_ref[slot[t], level, 0:c, 0:HEAD_DIM] = kt[:, 0:HEAD_DIM]
            kb_ref[slot[t], level, c:2 * c, HEAD_DIM:PAIR] = kt[:, HEAD_DIM:PAIR]
            return _dot_nt(qt, kb_ref[slot[t], level])

        att = {t: jnp.where(lvl == HGRN_LEVELS, scores(t, HGRN_LEVELS, None), 0.0) for t in tasks}
        for t in tasks:
            att[t] = jnp.where(lvl == 0, scores(t, 0, jnp.exp(jnp.where(odd_row, lf[t], 0.0))), att[t])
        for l in range(1, HGRN_LEVELS):
            for t in tasks:
                att[t] = jnp.where(lvl == l, scores(t, l, e_ref[slot[t], (l - 1) * c:l * c, :]), att[t])

        o2 = {t: _dot(att[t].astype(BF16), ib_ref[slot[t]]) for t in tasks}

        for t in tasks:
            u, p = t
            ecum = e_ref[slot[t], (HGRN_LEVELS - 1) * c:HGRN_LEVELS * c, :]
            glast = e_ref[slot[t], HGRN_LEVELS * c - 1:HGRN_LEVELS * c, :]
            qc = (q2[t] * ecum).astype(BF16)
            kr = (k2[t] * erem[t]).astype(BF16)
            gate = ng_ref[:, lanes[p]] * _sigmoid(g_ref[rows[u], lanes[p]])
            for hh in range(2):
                h = 2 * p + hh
                hl = slice(hh * HEAD_DIM, (hh + 1) * HEAD_DIM)
                st = st_ref[h]
                o = o2[t][:, hl] + _dot_nt(qc[:, hl], st.astype(BF16))
                st_ref[h] = st * glast[:, hl] + _dot_tn(i2[t][:, hl], kr[:, hl])
                o = o * lax.rsqrt(jnp.mean(o * o, axis=-1, keepdims=True) + LN_EPS)
                o_ref[rows[u], h * HEAD_DIM:(h + 1) * HEAD_DIM] = (o * gate[:, hl]).astype(o_ref.dtype)
        return carry

    lax.fori_loop(0, tt // (unroll * c), chunk, 0)


def _hgrn_mixer(proj, lower, norm_g, tt):
    b, t, _ = proj.shape
    mat, lvl = _hgrn_tables()
    c = HGRN_CHUNK
    blk = lambda n: pl.BlockSpec((None, tt, D_HGRN), lambda i, j: (i, j, HGRN_COL_BLOCK + n))
    vec = pl.BlockSpec((1, D_HGRN), lambda i, j: (0, 0))
    return pl.pallas_call(
        functools.partial(_hgrn_kernel, tt=tt),
        name="hgrn_mixer",
        grid=(b, t // tt),
        in_specs=[blk(0), blk(1), blk(2), blk(3), vec, vec,
                  pl.BlockSpec(mat.shape, lambda i, j: (0, 0)),
                  pl.BlockSpec(lvl.shape, lambda i, j: (0, 0))],
        out_specs=pl.BlockSpec((None, tt, D_HGRN), lambda i, j: (i, j, 0)),
        out_shape=jax.ShapeDtypeStruct((b, t, D_HGRN), BF16),
        scratch_shapes=[pltpu.VMEM((tt, D_HGRN), F32),
                        pltpu.VMEM((tt, D_HGRN), F32),
                        pltpu.VMEM((tt, D_HGRN), F32),
                        pltpu.VMEM((HGRN_UNROLL * HGRN_PAIRS, HGRN_LEVELS * c, PAIR), F32),
                        pltpu.VMEM((HGRN_UNROLL * HGRN_PAIRS, HGRN_LEVELS + 1, 2 * c, PAIR), BF16),
                        pltpu.VMEM((HGRN_UNROLL * HGRN_PAIRS, 2 * c, PAIR), BF16),
                        pltpu.VMEM((HGRN_HEADS, HEAD_DIM, HEAD_DIM), F32)],
        compiler_params=_params("parallel", "arbitrary"),
    )(proj, proj, proj, proj, lower.reshape(1, D_HGRN), norm_g.reshape(1, D_HGRN).astype(F32),
      jnp.asarray(mat, BF16), jnp.asarray(lvl))


def _ret_kernel(q_ref, k_ref, v_ref, g_ref, cos_ref, sin_ref, gg_ref, gb_ref, o_ref,
                intra_ref, qdec_ref, kdec_ref, st_ref, *, tt, c):
    log_gamma = [math.log1p(-(2.0 ** (-5.0 - h))) for h in range(RET_HEADS)]

    @pl.when(pl.program_id(1) == 0)
    def _():
        st_ref[...] = jnp.zeros(st_ref.shape, F32)
        row = lax.broadcasted_iota(jnp.int32, (c, c), 0)
        col = lax.broadcasted_iota(jnp.int32, (c, c), 1)
        diff = (row - col).astype(F32)
        rowl = lax.broadcasted_iota(jnp.int32, (c, HEAD_DIM), 0).astype(F32)
        for h in range(RET_HEADS):
            intra_ref[h] = jnp.where(diff >= 0, jnp.exp(log_gamma[h] * jnp.maximum(diff, 0.0)), 0.0)
            qdec_ref[h] = jnp.exp(log_gamma[h] * (rowl + 1.0))
            kdec_ref[h] = jnp.exp(log_gamma[h] * (c - 1.0 - rowl))

    k_scale = HEAD_DIM ** -0.5

    unroll = _chunks_per_trip(tt, c, RET_UNROLL)
    tasks = [(u, h) for u in range(unroll) for h in range(RET_HEADS)]
    cols = [slice(h * HEAD_DIM, (h + 1) * HEAD_DIM) for h in range(RET_HEADS)]

    def chunk(ci, carry):
        base = ci * (unroll * c)
        rows = [pl.ds(pl.multiple_of(base + u * c, c), c) for u in range(unroll)]
        cs = [cos_ref[rows[u], :] for u in range(unroll)]
        sn = [sin_ref[rows[u], :] for u in range(unroll)]
        qr, kr, vh, scores, outs = {}, {}, {}, {}, {}
        for t in tasks:
            u, h = t
            qh = q_ref[rows[u], cols[h]]
            kh = k_ref[rows[u], cols[h]]
            qr[t] = qh * cs[u] + pltpu.roll(qh, HEAD_DIM // 2, 1) * sn[u]
            kr[t] = (kh * cs[u] + pltpu.roll(kh, HEAD_DIM // 2, 1) * sn[u]) * k_scale
            vh[t] = v_ref[rows[u], cols[h]].astype(BF16)
            scores[t] = _dot_nt(qr[t].astype(BF16), kr[t].astype(BF16))
        for t in tasks:
            u, h = t
            st = st_ref[h]
            masked = (scores[t] * intra_ref[h]).astype(BF16)
            outs[t] = _dot(masked, vh[t]) + _dot((qr[t] * qdec_ref[h]).astype(BF16), st.astype(BF16))
            st_ref[h] = math.exp(log_gamma[h] * c) * st + _dot_tn((kr[t] * kdec_ref[h]).astype(BF16), vh[t])
        for t in tasks:
            u, h = t
            o = outs[t]
            mu = jnp.mean(o, axis=-1, keepdims=True)
            oc = o - mu
            var = jnp.mean(oc * oc, axis=-1, keepdims=True)
            on = oc * lax.rsqrt(var + LN_EPS) * gg_ref[:, cols[h]] + gb_ref[:, cols[h]]
            gh = g_ref[rows[u], cols[h]]
            o_ref[rows[u], cols[h]] = (gh * _sigmoid(gh) * on).astype(o_ref.dtype)
        return carry

    lax.fori_loop(0, tt // (unroll * c), chunk, 0)


def _ret_mixer(proj, cos_t, sin_t, gn_g, gn_b, tt):
    b, t, _ = proj.shape
    c = min(RET_CHUNK, tt)
    blk = lambda n: pl.BlockSpec((None, tt, D_RET), lambda i, j: (i, j, RET_COL_BLOCK + n))
    tab = pl.BlockSpec((None, tt, HEAD_DIM), lambda i, j: (i, j, 0))
    vec = pl.BlockSpec((1, D_RET), lambda i, j: (0, 0))
    return pl.pallas_call(
        functools.partial(_ret_kernel, tt=tt, c=c),
        name="ret_mixer",
        grid=(b, t // tt),
        in_specs=[blk(0), blk(1), blk(2), blk(3), tab, tab, vec, vec],
        out_specs=pl.BlockSpec((None, tt, D_RET), lambda i, j: (i, j, 0)),
        out_shape=jax.ShapeDtypeStruct((b, t, D_RET), BF16),
        scratch_shapes=[pltpu.VMEM((RET_HEADS, c, c), F32),
                        pltpu.VMEM((RET_HEADS, c, HEAD_DIM), F32),
                        pltpu.VMEM((RET_HEADS, c, HEAD_DIM), F32),
                        pltpu.VMEM((RET_HEADS, HEAD_DIM, HEAD_DIM), F32)],
        compiler_params=_params("parallel", "arbitrary"),
    )(proj, proj, proj, proj, cos_t, sin_t, gn_g.reshape(1, D_RET).astype(F32), gn_b.reshape(1, D_RET).astype(F32))


def _outproj_kernel(yc_ref, yh_ref, yr_ref, w_ref, x_ref, g_ref, b_ref, o_ref, mix_ref, *, alpha):
    mix_ref[:, 0:D_CONV] = yc_ref[...]
    mix_ref[:, D_CONV:D_CONV + D_HGRN] = yh_ref[...]
    mix_ref[:, D_CONV + D_HGRN:] = yr_ref[...]
    tm = o_ref.shape[0]
    rows = min(OUT_ROWS, tm)
    for r0 in range(0, tm, rows):
        z = alpha * x_ref[r0:r0 + rows, :] + _dot(mix_ref[r0:r0 + rows, :], w_ref[...])
        o_ref[r0:r0 + rows, :] = _layer_norm_rows(z, g_ref[...], b_ref[...])


def _outproj(yc, yh, yr, w_stack, layer, x2d, ln_g, ln_b, alpha, tm):
    m, d = x2d.shape
    d_mix = D_CONV + D_HGRN + D_RET
    vec = pl.BlockSpec((1, d), lambda i: (0, 0))
    return pl.pallas_call(
        functools.partial(_outproj_kernel, alpha=alpha),
        name="out_proj_ln",
        grid=(m // tm,),
        in_specs=[pl.BlockSpec((tm, D_CONV), lambda i: (i, 0)),
                  pl.BlockSpec((tm, D_HGRN), lambda i: (i, 0)),
                  pl.BlockSpec((tm, D_RET), lambda i: (i, 0)),
                  pl.BlockSpec((None, d_mix, d), lambda i: (layer, 0, 0)),
                  pl.BlockSpec((tm, d), lambda i: (i, 0)),
                  vec, vec],
        out_specs=pl.BlockSpec((tm, d), lambda i: (i, 0)),
        out_shape=jax.ShapeDtypeStruct((m, d), F32),
        scratch_shapes=[pltpu.VMEM((tm, d_mix), BF16)],
        compiler_params=_params("parallel"),
    )(yc, yh, yr, w_stack, x2d, ln_g.reshape(1, d).astype(F32), ln_b.reshape(1, d).astype(F32))


def _ffn_kernel(x_ref, w1_hbm, w2_hbm, g_ref, b_ref, o_ref, xb_ref, w1_buf, w2_buf, sem1, sem2, *,
                alpha, layer, tf):
    f = pl.program_id(1)
    nf = pl.num_programs(1)
    step = pl.program_id(0) * nf + f
    total = pl.num_programs(0) * nf

    def weight_copies(s):
        slot = s % FFN_SLOTS
        col0 = pl.multiple_of((s % nf) * tf, tf)
        return (pltpu.make_async_copy(w1_hbm.at[layer, :, pl.ds(col0, tf)], w1_buf.at[slot], sem1.at[slot]),
                pltpu.make_async_copy(w2_hbm.at[layer, pl.ds(col0, tf), :], w2_buf.at[slot], sem2.at[slot]))

    @pl.when(step == 0)
    def _():
        for s in range(FFN_SLOTS - 1):
            for copy in weight_copies(s):
                copy.start()

    @pl.when(step + (FFN_SLOTS - 1) < total)
    def _():
        for copy in weight_copies(step + (FFN_SLOTS - 1)):
            copy.start()

    for copy in weight_copies(step):
        copy.wait()
    w1_ref = w1_buf.at[step % FFN_SLOTS]
    w2_ref = w2_buf.at[step % FFN_SLOTS]

    @pl.when(f == 0)
    def _():
        xb_ref[...] = x_ref[...].astype(BF16)
        o_ref[...] = jnp.zeros(o_ref.shape, F32)

    last = nf - 1

    def hidden():
        hid = jnp.maximum(_dot(xb_ref[...], w1_ref[...].astype(BF16)), 0.0)
        return (hid * hid).astype(BF16)

    @pl.when(f < last)
    def _():
        o_ref[...] += _dot(hidden(), w2_ref[...].astype(BF16))

    @pl.when(f == last)
    def _():
        hid = hidden()
        w2 = w2_ref[...].astype(BF16)
        tm = o_ref.shape[0]
        rows = min(OUT_ROWS, tm)
        for r0 in range(0, tm, rows):
            z = alpha * x_ref[r0:r0 + rows, :] + (o_ref[r0:r0 + rows, :] + _dot(hid[r0:r0 + rows, :], w2))
            o_ref[r0:r0 + rows, :] = _layer_norm_rows(z, g_ref[...], b_ref[...])


def _ffn(x2d, w1_stack, w2_stack, layer, ln_g, ln_b, alpha, tm, tf):
    m, d = x2d.shape
    d_ff = w1_stack.shape[-1]
    grid = (m // tm, d_ff // tf)
    assert grid[0] * grid[1] >= FFN_SLOTS - 1
    vec = pl.BlockSpec((1, d), lambda i, f: (0, 0))
    return pl.pallas_call(
        functools.partial(_ffn_kernel, alpha=alpha, layer=layer, tf=tf),
        name="ffn_ln",
        grid=grid,
        in_specs=[pl.BlockSpec((tm, d), lambda i, f: (i, 0), pipeline_mode=pl.Buffered(1)),
                  pl.BlockSpec(memory_space=pl.ANY),
                  pl.BlockSpec(memory_space=pl.ANY),
                  vec, vec],
        out_specs=pl.BlockSpec((tm, d), lambda i, f: (i, 0), pipeline_mode=pl.Buffered(1)),
        out_shape=jax.ShapeDtypeStruct((m, d), F32),
        scratch_shapes=[pltpu.VMEM((tm, d), BF16),
                        pltpu.VMEM((FFN_SLOTS, d, tf), w1_stack.dtype),
                        pltpu.VMEM((FFN_SLOTS, tf, d), w2_stack.dtype),
                        pltpu.SemaphoreType.DMA((FFN_SLOTS,)),
                        pltpu.SemaphoreType.DMA((FFN_SLOTS,))],
        compiler_params=_params("arbitrary", "arbitrary"),
    )(x2d, w1_stack, w2_stack, ln_g.reshape(1, d).astype(F32), ln_b.reshape(1, d).astype(F32))


def kernel(x, positions, w_in, w_dw, b_dw, conv_ln_g, conv_ln_b, hgrn_lb, hgrn_norm_g, ret_gn_g, ret_gn_b,
           w_out, ln1_g, ln1_b, w_ff1, w_ff2, ln2_g, ln2_b):
    b, t, d = x.shape
    depth = w_in.shape[0]
    alpha = (2.0 * depth) ** 0.25
    m = b * t
    tt = min(512, t)
    tm = min(1024, m)
    tm_out = min(512, m)
    tn = 2 * D_CONV
    tf = 512

    w_in_b = w_in.astype(BF16)
    w_out_b = w_out.astype(BF16)

    lower = _lower_bounds(hgrn_lb)
    cos_t, sin_t = _rope_tables(positions, tt)

    x2d = x.reshape(m, d).astype(F32)
    for l in range(depth):
        proj = _proj(x2d, w_in_b, l, tm, tn).reshape(b, t, -1)
        yh = _hgrn_mixer(proj, lower[l], hgrn_norm_g[l], tt)
        yr = _ret_mixer(proj, cos_t, sin_t, ret_gn_g[l], ret_gn_b[l], tt)
        yc = _conv_mixer(proj, w_dw[l], b_dw[l], conv_ln_g[l], conv_ln_b[l], tt)
        x2d = _outproj(yc.reshape(m, D_CONV), yh.reshape(m, D_HGRN), yr.reshape(m, D_RET),
                       w_out_b, l, x2d, ln1_g[l], ln1_b[l], alpha, tm_out)
        x2d = _ffn(x2d, w_ff1, w_ff2, l, ln2_g[l], ln2_b[l], alpha, tm, tf)
    return x2d.reshape(b, t, d).astype(x.dtype)
```

```python
import functools
import math

import numpy as np
import jax
import jax.numpy as jnp
from jax import lax
from jax.experimental import pallas as pl
from jax.experimental.pallas import tpu as pltpu

F32 = jnp.float32
BF16 = jnp.bfloat16

D_CONV = 512
D_HGRN = 768
D_RET = 768
CONV_WIDTH = 31
HEAD_DIM = 128
HGRN_HEADS = D_HGRN // HEAD_DIM
RET_HEADS = D_RET // HEAD_DIM
ROPE_BASE = 10000.0
LN_EPS = 1e-5

LANES = 128
SUBLANES = 8
HGRN_CHUNK = 64
HGRN_LEVELS = 6
HGRN_PAIRS = HGRN_HEADS // 2
HGRN_UNROLL = 8
PAIR = 2 * HEAD_DIM
RET_CHUNK = 128
RET_UNROLL = 1
CONV_HALO = 32
CONV_ROWS = 128
NORM_ROWS = 64
FFN_SLOTS = 3
OUT_ROWS = 256

VMEM_LIMIT_BYTES = 56 * 1024 * 1024
FFN_VMEM_LIMIT_BYTES = 62 * 1024 * 1024

HGRN_COL_BLOCK = 0
RET_COL_BLOCK = 4 * D_HGRN // D_RET
CONV_COL_BLOCK = (4 * D_HGRN + 4 * D_RET) // D_CONV


def _dot(a, b):
    return jnp.dot(a, b, preferred_element_type=F32)


def _dot_nt(a, b):
    return lax.dot_general(a, b, (((1,), (1,)), ((), ())), preferred_element_type=F32)


def _dot_tn(a, b):
    return lax.dot_general(a, b, (((0,), (0,)), ((), ())), preferred_element_type=F32)


def _sigmoid(x):
    return 0.5 * jnp.tanh(0.5 * x) + 0.5


def _chunks_per_trip(tt, chunk, want):
    return max(u for u in range(1, want + 1) if tt % (u * chunk) == 0)


def _params(*semantics):
    return pltpu.CompilerParams(dimension_semantics=semantics, vmem_limit_bytes=VMEM_LIMIT_BYTES)


def _layer_norm_rows(z, g, b):
    mu = jnp.mean(z, axis=-1, keepdims=True)
    zc = z - mu
    var = jnp.mean(zc * zc, axis=-1, keepdims=True)
    return zc * lax.rsqrt(var + LN_EPS) * g + b


def _lower_bounds_kernel(lb_ref, o_ref):
    depth = lb_ref.shape[0]
    rows = [lb_ref[l:l + 1, :] for l in range(depth)]
    m = functools.reduce(jnp.maximum, rows)
    ex = [jnp.exp(r - m) for r in rows]
    tot = functools.reduce(jnp.add, ex)
    p = [e / tot for e in ex]
    c = p[0]
    for l in range(depth):
        if l > 0:
            c = c + p[l]
        o_ref[l:l + 1, :] = jnp.clip(c - p[0], 0.0, 1.0 - 1e-6)


def _lower_bounds(hgrn_lb):
    return pl.pallas_call(
        _lower_bounds_kernel,
        name="lower_bounds",
        out_shape=jax.ShapeDtypeStruct(hgrn_lb.shape, F32),
    )(hgrn_lb.astype(F32))


def _rope_kernel(pos_ref, inv_ref, cos_ref, sin_ref):
    ang = pos_ref[...].astype(F32) * inv_ref[...]
    lane = lax.broadcasted_iota(jnp.int32, ang.shape, 1)
    s = jnp.sin(ang)
    cos_ref[...] = jnp.cos(ang)
    sin_ref[...] = jnp.where(lane < HEAD_DIM // 2, -s, s)


def _rope_tables(positions, tt):
    b, t = positions.shape
    half = HEAD_DIM // 2
    inv = ROPE_BASE ** (-jnp.arange(half, dtype=F32) / half)
    inv2 = jnp.concatenate([inv, inv]).reshape(1, HEAD_DIM)
    tab = jax.ShapeDtypeStruct((b, t, HEAD_DIM), F32)
    return pl.pallas_call(
        _rope_kernel,
        name="rope_tables",
        grid=(b, t // tt),
        in_specs=[pl.BlockSpec((None, tt, 1), lambda i, j: (i, j, 0)),
                  pl.BlockSpec((1, HEAD_DIM), lambda i, j: (0, 0))],
        out_specs=[pl.BlockSpec((None, tt, HEAD_DIM), lambda i, j: (i, j, 0))] * 2,
        out_shape=[tab, tab],
        compiler_params=_params("parallel", "parallel"),
    )(positions.reshape(b, t, 1), inv2)


def _proj_kernel(x_ref, w_ref, o_ref, xb_ref):
    @pl.when(pl.program_id(1) == 0)
    def _():
        xb_ref[...] = x_ref[...].astype(BF16)

    o_ref[...] = _dot(xb_ref[...], w_ref[...])


def _proj(x2d, w_stack, layer, tm, tn):
    m, k = x2d.shape
    n = w_stack.shape[-1]
    n_blocks = n // tn
    first = 2 * D_CONV // tn
    return pl.pallas_call(
        _proj_kernel,
        name="in_proj",
        grid=(m // tm, n_blocks),
        in_specs=[pl.BlockSpec((tm, k), lambda i, j: (i, 0)),
                  pl.BlockSpec((None, k, tn), lambda i, j: (layer, 0, (j + first) % n_blocks))],
        out_specs=pl.BlockSpec((tm, tn), lambda i, j: (i, j)),
        out_shape=jax.ShapeDtypeStruct((m, n), F32),
        scratch_shapes=[pltpu.VMEM((tm, k), BF16)],
        compiler_params=_params("parallel", "arbitrary"),
    )(x2d, w_stack)


def _conv_kernel(a_ref, g_ref, w_ref, b_ref, lng_ref, lnb_ref, o_ref, u_ref, y_ref, *, tt):
    @pl.when(pl.program_id(1) == 0)
    def _():
        u_ref[0:CONV_HALO, :] = jnp.zeros((CONV_HALO, D_CONV), F32)

    @pl.when(pl.program_id(1) > 0)
    def _():
        u_ref[0:CONV_HALO, :] = u_ref[tt:tt + CONV_HALO, :]

    u_ref[CONV_HALO:CONV_HALO + tt, :] = a_ref[...] * _sigmoid(g_ref[...])

    first_tap = CONV_HALO - (CONV_WIDTH - 1)
    rows = min(CONV_ROWS, tt)
    win = rows + CONV_HALO
    for r0 in range(0, tt, rows):
        for c0 in range(0, D_CONV, LANES):
            acc = jnp.broadcast_to(b_ref[:, c0:c0 + LANES], (rows, LANES))
            for r in range(SUBLANES):
                window = u_ref[r0:r0 + win, c0:c0 + LANES]
                shifted = window if r == 0 else pltpu.roll(window, win - r, 0)
                for a in range(win // SUBLANES):
                    j = SUBLANES * a + r - first_tap
                    if 0 <= j < CONV_WIDTH:
                        acc = acc + w_ref[j:j + 1, c0:c0 + LANES] * shifted[SUBLANES * a:SUBLANES * a + rows]
            y_ref[r0:r0 + rows, c0:c0 + LANES] = acc

    nrows = min(NORM_ROWS, tt)
    for r0 in range(0, tt, nrows):
        v = _layer_norm_rows(y_ref[r0:r0 + nrows, :], lng_ref[...], lnb_ref[...])
        o_ref[r0:r0 + nrows, :] = (v * _sigmoid(v)).astype(o_ref.dtype)


def _conv_mixer(proj, w_dw, b_dw, ln_g, ln_b, tt):
    b, t, _ = proj.shape
    row = lambda v: v.reshape(1, D_CONV).astype(F32)
    vec = pl.BlockSpec((1, D_CONV), lambda i, j: (0, 0))
    blk = lambda n: pl.BlockSpec((None, tt, D_CONV), lambda i, j: (i, j, CONV_COL_BLOCK + n))
    return pl.pallas_call(
        functools.partial(_conv_kernel, tt=tt),
        name="conv_mixer",
        grid=(b, t // tt),
        in_specs=[blk(0), blk(1),
                  pl.BlockSpec((CONV_WIDTH, D_CONV), lambda i, j: (0, 0)),
                  vec, vec, vec],
        out_specs=pl.BlockSpec((None, tt, D_CONV), lambda i, j: (i, j, 0)),
        out_shape=jax.ShapeDtypeStruct((b, t, D_CONV), BF16),
        scratch_shapes=[pltpu.VMEM((CONV_HALO + tt, D_CONV), F32),
                        pltpu.VMEM((tt, D_CONV), F32)],
        compiler_params=_params("parallel", "arbitrary"),
    )(proj, proj, w_dw.astype(F32), row(b_dw), row(ln_g), row(ln_b))


def _hgrn_tables():
    c = HGRN_CHUNK
    mat = np.zeros((HGRN_LEVELS, c, c), np.float32)
    for l in range(1, HGRN_LEVELS):
        half = 1 << l
        for t in range(c):
            mid = ((t >> (l + 1)) << (l + 1)) + half - 1
            if t > mid:
                mat[l - 1, t, mid + 1:t + 1] = 1.0
            else:
                mat[l - 1, t, t + 1:mid + 1] = 1.0
    mat[HGRN_LEVELS - 1] = np.tril(np.ones((c, c), np.float32))
    mat = mat.reshape(-1, c)
    t = np.arange(c)[:, None]
    s = np.arange(c)[None, :]
    x = t ^ s
    lvl = np.zeros((c, c), np.int32)
    for l in range(1, HGRN_LEVELS):
        lvl += (x >= (1 << l)).astype(np.int32)
    lvl = np.where(t == s, HGRN_LEVELS, lvl)
    lvl = np.where(t < s, HGRN_LEVELS + 1, lvl)
    return np.concatenate([mat, mat, mat], axis=1), np.concatenate([lvl, lvl], axis=1).astype(np.int32)


def _hgrn_kernel(q_ref, f_ref, i_ref, g_ref, lb_ref, ng_ref, mat_ref, lvl_ref, o_ref,
                 qs_ref, ks_ref, lf_ref, e_ref, kb_ref, ib_ref, st_ref, *, tt):
    c = HGRN_CHUNK

    @pl.when(pl.program_id(1) == 0)
    def _():
        st_ref[...] = jnp.zeros(st_ref.shape, F32)
        kb_ref[...] = jnp.zeros(kb_ref.shape, BF16)
        ib_ref[...] = jnp.zeros(ib_ref.shape, BF16)

    q = q_ref[...]
    qs_ref[...] = q * _sigmoid(q)
    z = f_ref[...]
    ez = jnp.exp(-jnp.abs(z))
    inv = 1.0 / (1.0 + ez)
    log_sig = jnp.minimum(z, 0.0) - jnp.log(1.0 + ez)
    lb = lb_ref[...]
    f_pos = lb + (1.0 - lb) * (jnp.where(z >= 0, 1.0, ez) * inv)
    lf_ref[...] = jnp.where(lb > 0, jnp.log(f_pos), log_sig)
    ks_ref[...] = (1.0 - lb) * (jnp.where(z >= 0, ez, 1.0) * inv)

    lvl = lvl_ref[...]
    odd_row = (lax.broadcasted_iota(jnp.int32, (c, PAIR), 0) & 1) == 1

    unroll = _chunks_per_trip(tt, c, HGRN_UNROLL)
    tasks = [(u, p) for u in range(unroll) for p in range(HGRN_PAIRS)]

    def chunk(ci, carry):
        base = ci * (unroll * c)
        rows = [pl.ds(pl.multiple_of(base + u * c, c), c) for u in range(unroll)]
        lanes = [slice(p * PAIR, (p + 1) * PAIR) for p in range(HGRN_PAIRS)]
        slot = {t: n for n, t in enumerate(tasks)}

        lf, sums = {}, {}
        for t in tasks:
            u, p = t
            lf[t] = lf_ref[rows[u], lanes[p]]
            hi = lf[t].astype(BF16)
            r1 = lf[t] - hi.astype(F32)
            mid = r1.astype(BF16)
            lo = (r1 - mid.astype(F32)).astype(BF16)
            sums[t] = _dot(mat_ref[...], jnp.concatenate([hi, mid, lo], axis=0))

        q2, k2, i2, erem = {}, {}, {}, {}
        for t in tasks:
            u, p = t
            e_ref[slot[t]] = jnp.exp(sums[t])
            cum = sums[t][(HGRN_LEVELS - 1) * c:HGRN_LEVELS * c]
            erem[t] = jnp.exp(cum[c - 1:c, :] - cum)
            q2[t] = qs_ref[rows[u], lanes[p]]
            k2[t] = ks_ref[rows[u], lanes[p]]
            i2[t] = i_ref[rows[u], lanes[p]].astype(BF16)
            ib_ref[slot[t], 0:c, 0:HEAD_DIM] = i2[t][:, 0:HEAD_DIM]
            ib_ref[slot[t], c:2 * c, HEAD_DIM:PAIR] = i2[t][:, HEAD_DIM:PAIR]

        def scores(t, level, el):
            kt = (k2[t] * el).astype(BF16) if el is not None else k2[t].astype(BF16)
            qt = (q2[t] * el).astype(BF16) if el is not None else q2[t].astype(BF16)
            kb_ref[slot[t], level, 0:c, 0:HEAD_DIM] = kt[:, 0:HEAD_DIM]
            kb_ref[slot[t], level, c:2 * c, HEAD_DIM:PAIR] = kt[:, HEAD_DIM:PAIR]
            return _dot_nt(qt, kb_ref[slot[t], level])

        att = {t: jnp.where(lvl == HGRN_LEVELS, scores(t, HGRN_LEVELS, None), 0.0) for t in tasks}
        for t in tasks:
            att[t] = jnp.where(lvl == 0, scores(t, 0, jnp.exp(jnp.where(odd_row, lf[t], 0.0))), att[t])
        for l in range(1, HGRN_LEVELS):
            for t in tasks:
                att[t] = jnp.where(lvl == l, scores(t, l, e_ref[slot[t], (l - 1) * c:l * c, :]), att[t])

        o2 = {t: _dot(att[t].astype(BF16), ib_ref[slot[t]]) for t in tasks}

        for t in tasks:
            u, p = t
            ecum = e_ref[slot[t], (HGRN_LEVELS - 1) * c:HGRN_LEVELS * c, :]
            glast = e_ref[slot[t], HGRN_LEVELS * c - 1:HGRN_LEVELS * c, :]
            qc = (q2[t] * ecum).astype(BF16)
            kr = (k2[t] * erem[t]).astype(BF16)
            gate = ng_ref[:, lanes[p]] * _sigmoid(g_ref[rows[u], lanes[p]])
            for hh in range(2):
                h = 2 * p + hh
                hl = slice(hh * HEAD_DIM, (hh + 1) * HEAD_DIM)
                st = st_ref[h]
                o = o2[t][:, hl] + _dot_nt(qc[:, hl], st.astype(BF16))
                st_ref[h] = st * glast[:, hl] + _dot_tn(i2[t][:, hl], kr[:, hl])
                o = o * lax.rsqrt(jnp.mean(o * o, axis=-1, keepdims=True) + LN_EPS)
                o_ref[rows[u], h * HEAD_DIM:(h + 1) * HEAD_DIM] = (o * gate[:, hl]).astype(o_ref.dtype)
        return carry

    lax.fori_loop(0, tt // (unroll * c), chunk, 0)


def _hgrn_mixer(proj, lower, norm_g, tt):
    b, t, _ = proj.shape
    mat, lvl = _hgrn_tables()
    c = HGRN_CHUNK
    blk = lambda n: pl.BlockSpec((None, tt, D_HGRN), lambda i, j: (i, j, HGRN_COL_BLOCK + n))
    vec = pl.BlockSpec((1, D_HGRN), lambda i, j: (0, 0))
    return pl.pallas_call(
        functools.partial(_hgrn_kernel, tt=tt),
        name="hgrn_mixer",
        grid=(b, t // tt),
        in_specs=[blk(0), blk(1), blk(2), blk(3), vec, vec,
                  pl.BlockSpec(mat.shape, lambda i, j: (0, 0)),
                  pl.BlockSpec(lvl.shape, lambda i, j: (0, 0))],
        out_specs=pl.BlockSpec((None, tt, D_HGRN), lambda i, j: (i, j, 0)),
        out_shape=jax.ShapeDtypeStruct((b, t, D_HGRN), BF16),
        scratch_shapes=[pltpu.VMEM((tt, D_HGRN), F32),
                        pltpu.VMEM((tt, D_HGRN), F32),
                        pltpu.VMEM((tt, D_HGRN), F32),
                        pltpu.VMEM((HGRN_UNROLL * HGRN_PAIRS, HGRN_LEVELS * c, PAIR), F32),
                        pltpu.VMEM((HGRN_UNROLL * HGRN_PAIRS, HGRN_LEVELS + 1, 2 * c, PAIR), BF16),
                        pltpu.VMEM((HGRN_UNROLL * HGRN_PAIRS, 2 * c, PAIR), BF16),
                        pltpu.VMEM((HGRN_HEADS, HEAD_DIM, HEAD_DIM), F32)],
        compiler_params=_params("parallel", "arbitrary"),
    )(proj, proj, proj, proj, lower.reshape(1, D_HGRN), norm_g.reshape(1, D_HGRN).astype(F32),
      jnp.asarray(mat, BF16), jnp.asarray(lvl))


def _ret_kernel(q_ref, k_ref, v_ref, g_ref, cos_ref, sin_ref, gg_ref, gb_ref, o_ref,
                intra_ref, qdec_ref, kdec_ref, st_ref, *, tt, c):
    log_gamma = [math.log1p(-(2.0 ** (-5.0 - h))) for h in range(RET_HEADS)]

    @pl.when(pl.program_id(1) == 0)
    def _():
        st_ref[...] = jnp.zeros(st_ref.shape, F32)
        row = lax.broadcasted_iota(jnp.int32, (c, c), 0)
        col = lax.broadcasted_iota(jnp.int32, (c, c), 1)
        diff = (row - col).astype(F32)
        rowl = lax.broadcasted_iota(jnp.int32, (c, HEAD_DIM), 0).astype(F32)
        for h in range(RET_HEADS):
            intra_ref[h] = jnp.where(diff >= 0, jnp.exp(log_gamma[h] * jnp.maximum(diff, 0.0)), 0.0)
            qdec_ref[h] = jnp.exp(log_gamma[h] * (rowl + 1.0))
            kdec_ref[h] = jnp.exp(log_gamma[h] * (c - 1.0 - rowl))

    k_scale = HEAD_DIM ** -0.5

    unroll = _chunks_per_trip(tt, c, RET_UNROLL)
    tasks = [(u, h) for u in range(unroll) for h in range(RET_HEADS)]
    cols = [slice(h * HEAD_DIM, (h + 1) * HEAD_DIM) for h in range(RET_HEADS)]

    def chunk(ci, carry):
        base = ci * (unroll * c)
        rows = [pl.ds(pl.multiple_of(base + u * c, c), c) for u in range(unroll)]
        cs = [cos_ref[rows[u], :] for u in range(unroll)]
        sn = [sin_ref[rows[u], :] for u in range(unroll)]
        qr, kr, vh, scores, outs = {}, {}, {}, {}, {}
        for t in tasks:
            u, h = t
            qh = q_ref[rows[u], cols[h]]
            kh = k_ref[rows[u], cols[h]]
            qr[t] = qh * cs[u] + pltpu.roll(qh, HEAD_DIM // 2, 1) * sn[u]
            kr[t] = (kh * cs[u] + pltpu.roll(kh, HEAD_DIM // 2, 1) * sn[u]) * k_scale
            vh[t] = v_ref[rows[u], cols[h]].astype(BF16)
            scores[t] = _dot_nt(qr[t].astype(BF16), kr[t].astype(BF16))
        for t in tasks:
            u, h = t
            st = st_ref[h]
            masked = (scores[t] * intra_ref[h]).astype(BF16)
            outs[t] = _dot(masked, vh[t]) + _dot((qr[t] * qdec_ref[h]).astype(BF16), st.astype(BF16))
            st_ref[h] = math.exp(log_gamma[h] * c) * st + _dot_tn((kr[t] * kdec_ref[h]).astype(BF16), vh[t])
        for t in tasks:
            u, h = t
            o = outs[t]
            mu = jnp.mean(o, axis=-1, keepdims=True)
            oc = o - mu
            var = jnp.mean(oc * oc, axis=-1, keepdims=True)
            on = oc * lax.rsqrt(var + LN_EPS) * gg_ref[:, cols[h]] + gb_ref[:, cols[h]]
            gh = g_ref[rows[u], cols[h]]
            o_ref[rows[u], cols[h]] = (gh * _sigmoid(gh) * on).astype(o_ref.dtype)
        return carry

    lax.fori_loop(0, tt // (unroll * c), chunk, 0)


def _ret_mixer(proj, cos_t, sin_t, gn_g, gn_b, tt):
    b, t, _ = proj.shape
    c = min(RET_CHUNK, tt)
    blk = lambda n: pl.BlockSpec((None, tt, D_RET), lambda i, j: (i, j, RET_COL_BLOCK + n))
    tab = pl.BlockSpec((None, tt, HEAD_DIM), lambda i, j: (i, j, 0))
    vec = pl.BlockSpec((1, D_RET), lambda i, j: (0, 0))
    return pl.pallas_call(
        functools.partial(_ret_kernel, tt=tt, c=c),
        name="ret_mixer",
        grid=(b, t // tt),
        in_specs=[blk(0), blk(1), blk(2), blk(3), tab, tab, vec, vec],
        out_specs=pl.BlockSpec((None, tt, D_RET), lambda i, j: (i, j, 0)),
        out_shape=jax.ShapeDtypeStruct((b, t, D_RET), BF16),
        scratch_shapes=[pltpu.VMEM((RET_HEADS, c, c), F32),
                        pltpu.VMEM((RET_HEADS, c, HEAD_DIM), F32),
                        pltpu.VMEM((RET_HEADS, c, HEAD_DIM), F32),
                        pltpu.VMEM((RET_HEADS, HEAD_DIM, HEAD_DIM), F32)],
        compiler_params=_params("parallel", "arbitrary"),
    )(proj, proj, proj, proj, cos_t, sin_t, gn_g.reshape(1, D_RET).astype(F32), gn_b.reshape(1, D_RET).astype(F32))


def _outproj_kernel(yc_ref, yh_ref, yr_ref, w_ref, x_ref, g_ref, b_ref, o_ref, mix_ref, *, alpha):
    mix_ref[:, 0:D_CONV] = yc_ref[...]
    mix_ref[:, D_CONV:D_CONV + D_HGRN] = yh_ref[...]
    mix_ref[:, D_CONV + D_HGRN:] = yr_ref[...]
    tm = o_ref.shape[0]
    rows = min(OUT_ROWS, tm)
    for r0 in range(0, tm, rows):
        z = alpha * x_ref[r0:r0 + rows, :] + _dot(mix_ref[r0:r0 + rows, :], w_ref[...])
        o_ref[r0:r0 + rows, :] = _layer_norm_rows(z, g_ref[...], b_ref[...])


def _outproj(yc, yh, yr, w_stack, layer, x2d, ln_g, ln_b, alpha, tm):
    m, d = x2d.shape
    d_mix = D_CONV + D_HGRN + D_RET
    vec = pl.BlockSpec((1, d), lambda i: (0, 0))
    return pl.pallas_call(
        functools.partial(_outproj_kernel, alpha=alpha),
        name="out_proj_ln",
        grid=(m // tm,),
        in_specs=[pl.BlockSpec((tm, D_CONV), lambda i: (i, 0)),
                  pl.BlockSpec((tm, D_HGRN), lambda i: (i, 0)),
                  pl.BlockSpec((tm, D_RET), lambda i: (i, 0)),
                  pl.BlockSpec((None, d_mix, d), lambda i: (layer, 0, 0)),
                  pl.BlockSpec((tm, d), lambda i: (i, 0)),
                  vec, vec],
        out_specs=pl.BlockSpec((tm, d), lambda i: (i, 0)),
        out_shape=jax.ShapeDtypeStruct((m, d), F32),
        scratch_shapes=[pltpu.VMEM((tm, d_mix), BF16)],
        compiler_params=_params("parallel"),
    )(yc, yh, yr, w_stack, x2d, ln_g.reshape(1, d).astype(F32), ln_b.reshape(1, d).astype(F32))


def _ffn_kernel(x_ref, w1_hbm, w2_hbm, g_ref, b_ref, o_ref, xb_ref, w1_buf, w2_buf, sem1, sem2, *,
                alpha, layer, tf):
    f = pl.program_id(1)
    nf = pl.num_programs(1)
    step = pl.program_id(0) * nf + f
    total = pl.num_programs(0) * nf

    def weight_copies(s):
        slot = s % FFN_SLOTS
        col0 = pl.multiple_of((s % nf) * tf, tf)
        return (pltpu.make_async_copy(w1_hbm.at[layer, :, pl.ds(col0, tf)], w1_buf.at[slot], sem1.at[slot]),
                pltpu.make_async_copy(w2_hbm.at[layer, pl.ds(col0, tf), :], w2_buf.at[slot], sem2.at[slot]))

    @pl.when(step == 0)
    def _():
        for s in range(FFN_SLOTS - 1):
            for copy in weight_copies(s):
                copy.start()

    @pl.when(step + (FFN_SLOTS - 1) < total)
    def _():
        for copy in weight_copies(step + (FFN_SLOTS - 1)):
            copy.start()

    for copy in weight_copies(step):
        copy.wait()
    w1_ref = w1_buf.at[step % FFN_SLOTS]
    w2_ref = w2_buf.at[step % FFN_SLOTS]

    @pl.when(f == 0)
    def _():
        xb_ref[...] = x_ref[...].astype(BF16)
        o_ref[...] = jnp.zeros(o_ref.shape, F32)

    last = nf - 1

    def hidden():
        hid = jnp.maximum(_dot(xb_ref[...], w1_ref[...].astype(BF16)), 0.0)
        return (hid * hid).astype(BF16)

    @pl.when(f < last)
    def _():
        o_ref[...] += _dot(hidden(), w2_ref[...].astype(BF16))

    @pl.when(f == last)
    def _():
        hid = hidden()
        w2 = w2_ref[...].astype(BF16)
        tm = o_ref.shape[0]
        rows = min(OUT_ROWS, tm)
        for r0 in range(0, tm, rows):
            z = alpha * x_ref[r0:r0 + rows, :] + (o_ref[r0:r0 + rows, :] + _dot(hid[r0:r0 + rows, :], w2))
            o_ref[r0:r0 + rows, :] = _layer_norm_rows(z, g_ref[...], b_ref[...])


def _ffn(x2d, w1_stack, w2_stack, layer, ln_g, ln_b, alpha, tm, tf):
    m, d = x2d.shape
    d_ff = w1_stack.shape[-1]
    grid = (m // tm, d_ff // tf)
    assert grid[0] * grid[1] >= FFN_SLOTS - 1
    vec = pl.BlockSpec((1, d), lambda i, f: (0, 0))
    return pl.pallas_call(
        functools.partial(_ffn_kernel, alpha=alpha, layer=layer, tf=tf),
        name="ffn_ln",
        grid=grid,
        in_specs=[pl.BlockSpec((tm, d), lambda i, f: (i, 0)),
                  pl.BlockSpec(memory_space=pl.ANY),
                  pl.BlockSpec(memory_space=pl.ANY),
                  vec, vec],
        out_specs=pl.BlockSpec((tm, d), lambda i, f: (i, 0), pipeline_mode=pl.Buffered(1)),
        out_shape=jax.ShapeDtypeStruct((m, d), F32),
        scratch_shapes=[pltpu.VMEM((tm, d), BF16),
                        pltpu.VMEM((FFN_SLOTS, d, tf), w1_stack.dtype),
                        pltpu.VMEM((FFN_SLOTS, tf, d), w2_stack.dtype),
                        pltpu.SemaphoreType.DMA((FFN_SLOTS,)),
                        pltpu.SemaphoreType.DMA((FFN_SLOTS,))],
        compiler_params=pltpu.CompilerParams(dimension_semantics=("arbitrary", "arbitrary"),
                                             vmem_limit_bytes=FFN_VMEM_LIMIT_BYTES),
    )(x2d, w1_stack, w2_stack, ln_g.reshape(1, d).astype(F32), ln_b.reshape(1, d).astype(F32))


def kernel(x, positions, w_in, w_dw, b_dw, conv_ln_g, conv_ln_b, hgrn_lb, hgrn_norm_g, ret_gn_g, ret_gn_b,
           w_out, ln1_g, ln1_b, w_ff1, w_ff2, ln2_g, ln2_b):
    b, t, d = x.shape
    depth = w_in.shape[0]
    alpha = (2.0 * depth) ** 0.25
    m = b * t
    tt = min(512, t)
    tm = min(1024, m)
    tm_out = min(512, m)
    tn = 2 * D_CONV
    tf = 512

    w_in_b = w_in.astype(BF16)
    w_out_b = w_out.astype(BF16)

    lower = _lower_bounds(hgrn_lb)
    cos_t, sin_t = _rope_tables(positions, tt)

    x2d = x.reshape(m, d).astype(F32)
    for l in range(depth):
        proj = _proj(x2d, w_in_b, l, tm, tn).reshape(b, t, -1)
        yh = _hgrn_mixer(proj, lower[l], hgrn_norm_g[l], tt)
        yr = _ret_mixer(proj, cos_t, sin_t, ret_gn_g[l], ret_gn_b[l], tt)
        yc = _conv_mixer(proj, w_dw[l], b_dw[l], conv_ln_g[l], conv_ln_b[l], tt)
        x2d = _outproj(yc.reshape(m, D_CONV), yh.reshape(m, D_HGRN), yr.reshape(m, D_RET),
                       w_out_b, l, x2d, ln1_g[l], ln1_b[l], alpha, tm_out)
        x2d = _ffn(x2d, w_ff1, w_ff2, l, ln2_g[l], ln2_b[l], alpha, tm, tf)
    return x2d.reshape(b, t, d).astype(x.dtype)
```

```python
import functools
import math

import numpy as np
import jax
import jax.numpy as jnp
from jax import lax
from jax.experimental import pallas as pl
from jax.experimental.pallas import tpu as pltpu

F32 = jnp.float32
BF16 = jnp.bfloat16

D_CONV = 512
D_HGRN = 768
D_RET = 768
CONV_WIDTH = 31
HEAD_DIM = 128
HGRN_HEADS = D_HGRN // HEAD_DIM
RET_HEADS = D_RET // HEAD_DIM
ROPE_BASE = 10000.0
LN_EPS = 1e-5

LANES = 128
SUBLANES = 8
HGRN_CHUNK = 64
HGRN_LEVELS = 6
HGRN_PAIRS = HGRN_HEADS // 2
HGRN_UNROLL = 8
PAIR = 2 * HEAD_DIM
RET_CHUNK = 128
RET_UNROLL = 1
CONV_HALO = 32
CONV_ROWS = 128
NORM_ROWS = 64
OUT_ROWS = 256

VMEM_LIMIT_BYTES = 56 * 1024 * 1024
FFN_VMEM_LIMIT_BYTES = 62 * 1024 * 1024

HGRN_COL_BLOCK = 0
RET_COL_BLOCK = 4 * D_HGRN // D_RET
CONV_COL_BLOCK = (4 * D_HGRN + 4 * D_RET) // D_CONV


def _dot(a, b):
    return jnp.dot(a, b, preferred_element_type=F32)


def _dot_nt(a, b):
    return lax.dot_general(a, b, (((1,), (1,)), ((), ())), preferred_element_type=F32)


def _dot_tn(a, b):
    return lax.dot_general(a, b, (((0,), (0,)), ((), ())), preferred_element_type=F32)


def _sigmoid(x):
    return 0.5 * jnp.tanh(0.5 * x) + 0.5


def _chunks_per_trip(tt, chunk, want):
    return max(u for u in range(1, want + 1) if tt % (u * chunk) == 0)


def _params(*semantics):
    return pltpu.CompilerParams(dimension_semantics=semantics, vmem_limit_bytes=VMEM_LIMIT_BYTES)


def _layer_norm_rows(z, g, b):
    mu = jnp.mean(z, axis=-1, keepdims=True)
    zc = z - mu
    var = jnp.mean(zc * zc, axis=-1, keepdims=True)
    return zc * lax.rsqrt(var + LN_EPS) * g + b


def _lower_bounds_kernel(lb_ref, o_ref):
    depth = lb_ref.shape[0]
    rows = [lb_ref[l:l + 1, :] for l in range(depth)]
    m = functools.reduce(jnp.maximum, rows)
    ex = [jnp.exp(r - m) for r in rows]
    tot = functools.reduce(jnp.add, ex)
    p = [e / tot for e in ex]
    c = p[0]
    for l in range(depth):
        if l > 0:
            c = c + p[l]
        o_ref[l:l + 1, :] = jnp.clip(c - p[0], 0.0, 1.0 - 1e-6)


def _lower_bounds(hgrn_lb):
    return pl.pallas_call(
        _lower_bounds_kernel,
        name="lower_bounds",
        out_shape=jax.ShapeDtypeStruct(hgrn_lb.shape, F32),
    )(hgrn_lb.astype(F32))


def _rope_kernel(pos_ref, inv_ref, cos_ref, sin_ref):
    ang = pos_ref[...].astype(F32) * inv_ref[...]
    lane = lax.broadcasted_iota(jnp.int32, ang.shape, 1)
    s = jnp.sin(ang)
    cos_ref[...] = jnp.cos(ang)
    sin_ref[...] = jnp.where(lane < HEAD_DIM // 2, -s, s)


def _rope_tables(positions, tt):
    b, t = positions.shape
    half = HEAD_DIM // 2
    inv = ROPE_BASE ** (-jnp.arange(half, dtype=F32) / half)
    inv2 = jnp.concatenate([inv, inv]).reshape(1, HEAD_DIM)
    tab = jax.ShapeDtypeStruct((b, t, HEAD_DIM), F32)
    return pl.pallas_call(
        _rope_kernel,
        name="rope_tables",
        grid=(b, t // tt),
        in_specs=[pl.BlockSpec((None, tt, 1), lambda i, j: (i, j, 0)),
                  pl.BlockSpec((1, HEAD_DIM), lambda i, j: (0, 0))],
        out_specs=[pl.BlockSpec((None, tt, HEAD_DIM), lambda i, j: (i, j, 0))] * 2,
        out_shape=[tab, tab],
        compiler_params=_params("parallel", "parallel"),
    )(positions.reshape(b, t, 1), inv2)


def _proj_kernel(x_ref, w_ref, o_ref, xb_ref):
    @pl.when(pl.program_id(1) == 0)
    def _():
        xb_ref[...] = x_ref[...].astype(BF16)

    o_ref[...] = _dot(xb_ref[...], w_ref[...])


def _proj(x2d, w_stack, layer, tm, tn):
    m, k = x2d.shape
    n = w_stack.shape[-1]
    n_blocks = n // tn
    first = 2 * D_CONV // tn
    return pl.pallas_call(
        _proj_kernel,
        name="in_proj",
        grid=(m // tm, n_blocks),
        in_specs=[pl.BlockSpec((tm, k), lambda i, j: (i, 0)),
                  pl.BlockSpec((None, k, tn), lambda i, j: (layer, 0, (j + first) % n_blocks))],
        out_specs=pl.BlockSpec((tm, tn), lambda i, j: (i, j)),
        out_shape=jax.ShapeDtypeStruct((m, n), F32),
        scratch_shapes=[pltpu.VMEM((tm, k), BF16)],
        compiler_params=_params("parallel", "arbitrary"),
    )(x2d, w_stack)


def _conv_kernel(a_ref, g_ref, w_ref, b_ref, lng_ref, lnb_ref, o_ref, u_ref, y_ref, *, tt):
    @pl.when(pl.program_id(1) == 0)
    def _():
        u_ref[0:CONV_HALO, :] = jnp.zeros((CONV_HALO, D_CONV), F32)

    @pl.when(pl.program_id(1) > 0)
    def _():
        u_ref[0:CONV_HALO, :] = u_ref[tt:tt + CONV_HALO, :]

    u_ref[CONV_HALO:CONV_HALO + tt, :] = a_ref[...] * _sigmoid(g_ref[...])

    first_tap = CONV_HALO - (CONV_WIDTH - 1)
    rows = min(CONV_ROWS, tt)
    win = rows + CONV_HALO
    for r0 in range(0, tt, rows):
        for c0 in range(0, D_CONV, LANES):
            acc = jnp.broadcast_to(b_ref[:, c0:c0 + LANES], (rows, LANES))
            for r in range(SUBLANES):
                window = u_ref[r0:r0 + win, c0:c0 + LANES]
                shifted = window if r == 0 else pltpu.roll(window, win - r, 0)
                for a in range(win // SUBLANES):
                    j = SUBLANES * a + r - first_tap
                    if 0 <= j < CONV_WIDTH:
                        acc = acc + w_ref[j:j + 1, c0:c0 + LANES] * shifted[SUBLANES * a:SUBLANES * a + rows]
            y_ref[r0:r0 + rows, c0:c0 + LANES] = acc

    nrows = min(NORM_ROWS, tt)
    for r0 in range(0, tt, nrows):
        v = _layer_norm_rows(y_ref[r0:r0 + nrows, :], lng_ref[...], lnb_ref[...])
        o_ref[r0:r0 + nrows, :] = (v * _sigmoid(v)).astype(o_ref.dtype)


def _conv_mixer(proj, w_dw, b_dw, ln_g, ln_b, tt):
    b, t, _ = proj.shape
    row = lambda v: v.reshape(1, D_CONV).astype(F32)
    vec = pl.BlockSpec((1, D_CONV), lambda i, j: (0, 0))
    blk = lambda n: pl.BlockSpec((None, tt, D_CONV), lambda i, j: (i, j, CONV_COL_BLOCK + n))
    return pl.pallas_call(
        functools.partial(_conv_kernel, tt=tt),
        name="conv_mixer",
        grid=(b, t // tt),
        in_specs=[blk(0), blk(1),
                  pl.BlockSpec((CONV_WIDTH, D_CONV), lambda i, j: (0, 0)),
                  vec, vec, vec],
        out_specs=pl.BlockSpec((None, tt, D_CONV), lambda i, j: (i, j, 0)),
        out_shape=jax.ShapeDtypeStruct((b, t, D_CONV), BF16),
        scratch_shapes=[pltpu.VMEM((CONV_HALO + tt, D_CONV), F32),
                        pltpu.VMEM((tt, D_CONV), F32)],
        compiler_params=_params("parallel", "arbitrary"),
    )(proj, proj, w_dw.astype(F32), row(b_dw), row(ln_g), row(ln_b))


def _hgrn_tables():
    c = HGRN_CHUNK
    mat = np.zeros((HGRN_LEVELS, c, c), np.float32)
    for l in range(1, HGRN_LEVELS):
        half = 1 << l
        for t in range(c):
            mid = ((t >> (l + 1)) << (l + 1)) + half - 1
            if t > mid:
                mat[l - 1, t, mid + 1:t + 1] = 1.0
            else:
                mat[l - 1, t, t + 1:mid + 1] = 1.0
    mat[HGRN_LEVELS - 1] = np.tril(np.ones((c, c), np.float32))
    mat = mat.reshape(-1, c)
    t = np.arange(c)[:, None]
    s = np.arange(c)[None, :]
    x = t ^ s
    lvl = np.zeros((c, c), np.int32)
    for l in range(1, HGRN_LEVELS):
        lvl += (x >= (1 << l)).astype(np.int32)
    lvl = np.where(t == s, HGRN_LEVELS, lvl)
    lvl = np.where(t < s, HGRN_LEVELS + 1, lvl)
    return np.concatenate([mat, mat, mat], axis=1), np.concatenate([lvl, lvl], axis=1).astype(np.int32)


def _hgrn_kernel(q_ref, f_ref, i_ref, g_ref, lb_ref, ng_ref, mat_ref, lvl_ref, o_ref,
                 qs_ref, ks_ref, lf_ref, e_ref, kb_ref, ib_ref, st_ref, *, tt):
    c = HGRN_CHUNK

    @pl.when(pl.program_id(1) == 0)
    def _():
        st_ref[...] = jnp.zeros(st_ref.shape, F32)
        kb_ref[...] = jnp.zeros(kb_ref.shape, BF16)
        ib_ref[...] = jnp.zeros(ib_ref.shape, BF16)

    q = q_ref[...]
    qs_ref[...] = q * _sigmoid(q)
    z = f_ref[...]
    ez = jnp.exp(-jnp.abs(z))
    inv = 1.0 / (1.0 + ez)
    log_sig = jnp.minimum(z, 0.0) - jnp.log(1.0 + ez)
    lb = lb_ref[...]
    f_pos = lb + (1.0 - lb) * (jnp.where(z >= 0, 1.0, ez) * inv)
    lf_ref[...] = jnp.where(lb > 0, jnp.log(f_pos), log_sig)
    ks_ref[...] = (1.0 - lb) * (jnp.where(z >= 0, ez, 1.0) * inv)

    lvl = lvl_ref[...]
    odd_row = (lax.broadcasted_iota(jnp.int32, (c, PAIR), 0) & 1) == 1

    unroll = _chunks_per_trip(tt, c, HGRN_UNROLL)
    tasks = [(u, p) for u in range(unroll) for p in range(HGRN_PAIRS)]

    def chunk(ci, carry):
        base = ci * (unroll * c)
        rows = [pl.ds(pl.multiple_of(base + u * c, c), c) for u in range(unroll)]
        lanes = [slice(p * PAIR, (p + 1) * PAIR) for p in range(HGRN_PAIRS)]
        slot = {t: n for n, t in enumerate(tasks)}

        lf, sums = {}, {}
        for t in tasks:
            u, p = t
            lf[t] = lf_ref[rows[u], lanes[p]]
            hi = lf[t].astype(BF16)
            r1 = lf[t] - hi.astype(F32)
            mid = r1.astype(BF16)
            lo = (r1 - mid.astype(F32)).astype(BF16)
            sums[t] = _dot(mat_ref[...], jnp.concatenate([hi, mid, lo], axis=0))

        q2, k2, i2, erem = {}, {}, {}, {}
        for t in tasks:
            u, p = t
            e_ref[slot[t]] = jnp.exp(sums[t])
            cum = sums[t][(HGRN_LEVELS - 1) * c:HGRN_LEVELS * c]
            erem[t] = jnp.exp(cum[c - 1:c, :] - cum)
            q2[t] = qs_ref[rows[u], lanes[p]]
            k2[t] = ks_ref[rows[u], lanes[p]]
            i2[t] = i_ref[rows[u], lanes[p]].astype(BF16)
            ib_ref[slot[t], 0:c, 0:HEAD_DIM] = i2[t][:, 0:HEAD_DIM]
            ib_ref[slot[t], c:2 * c, HEAD_DIM:PAIR] = i2[t][:, HEAD_DIM:PAIR]

        def scores(t, level, el):
            kt = (k2[t] * el).astype(BF16) if el is not None else k2[t].astype(BF16)
            qt = (q2[t] * el).astype(BF16) if el is not None else q2[t].astype(BF16)
            kb_ref[slot[t], level, 0:c, 0:HEAD_DIM] = kt[:, 0:HEAD_DIM]
            kb_ref[slot[t], level, c:2 * c, HEAD_DIM:PAIR] = kt[:, HEAD_DIM:PAIR]
            return _dot_nt(qt, kb_ref[slot[t], level])

        att = {t: jnp.where(lvl == HGRN_LEVELS, scores(t, HGRN_LEVELS, None), 0.0) for t in tasks}
        for t in tasks:
            att[t] = jnp.where(lvl == 0, scores(t, 0, jnp.exp(jnp.where(odd_row, lf[t], 0.0))), att[t])
        for l in range(1, HGRN_LEVELS):
            for t in tasks:
                att[t] = jnp.where(lvl == l, scores(t, l, e_ref[slot[t], (l - 1) * c:l * c, :]), att[t])

        o2 = {t: _dot(att[t].astype(BF16), ib_ref[slot[t]]) for t in tasks}

        for t in tasks:
            u, p = t
            ecum = e_ref[slot[t], (HGRN_LEVELS - 1) * c:HGRN_LEVELS * c, :]
            glast = e_ref[slot[t], HGRN_LEVELS * c - 1:HGRN_LEVELS * c, :]
            qc = (q2[t] * ecum).astype(BF16)
            kr = (k2[t] * erem[t]).astype(BF16)
            gate = ng_ref[:, lanes[p]] * _sigmoid(g_ref[rows[u], lanes[p]])
            for hh in range(2):
                h = 2 * p + hh
                hl = slice(hh * HEAD_DIM, (hh + 1) * HEAD_DIM)
                st = st_ref[h]
                o = o2[t][:, hl] + _dot_nt(qc[:, hl], st.astype(BF16))
                st_ref[h] = st * glast[:, hl] + _dot_tn(i2[t][:, hl], kr[:, hl])
                o = o * lax.rsqrt(jnp.mean(o * o, axis=-1, keepdims=True) + LN_EPS)
                o_ref[rows[u], h * HEAD_DIM:(h + 1) * HEAD_DIM] = (o * gate[:, hl]).astype(o_ref.dtype)
        return carry

    lax.fori_loop(0, tt // (unroll * c), chunk, 0)


def _hgrn_mixer(proj, lower, norm_g, tt):
    b, t, _ = proj.shape
    mat, lvl = _hgrn_tables()
    c = HGRN_CHUNK
    blk = lambda n: pl.BlockSpec((None, tt, D_HGRN), lambda i, j: (i, j, HGRN_COL_BLOCK + n))
    vec = pl.BlockSpec((1, D_HGRN), lambda i, j: (0, 0))
    return pl.pallas_call(
        functools.partial(_hgrn_kernel, tt=tt),
        name="hgrn_mixer",
        grid=(b, t // tt),
        in_specs=[blk(0), blk(1), blk(2), blk(3), vec, vec,
                  pl.BlockSpec(mat.shape, lambda i, j: (0, 0)),
                  pl.BlockSpec(lvl.shape, lambda i, j: (0, 0))],
        out_specs=pl.BlockSpec((None, tt, D_HGRN), lambda i, j: (i, j, 0)),
        out_shape=jax.ShapeDtypeStruct((b, t, D_HGRN), BF16),
        scratch_shapes=[pltpu.VMEM((tt, D_HGRN), F32),
                        pltpu.VMEM((tt, D_HGRN), F32),
                        pltpu.VMEM((tt, D_HGRN), F32),
                        pltpu.VMEM((HGRN_UNROLL * HGRN_PAIRS, HGRN_LEVELS * c, PAIR), F32),
                        pltpu.VMEM((HGRN_UNROLL * HGRN_PAIRS, HGRN_LEVELS + 1, 2 * c, PAIR), BF16),
                        pltpu.VMEM((HGRN_UNROLL * HGRN_PAIRS, 2 * c, PAIR), BF16),
                        pltpu.VMEM((HGRN_HEADS, HEAD_DIM, HEAD_DIM), F32)],
        compiler_params=_params("parallel", "arbitrary"),
    )(proj, proj, proj, proj, lower.reshape(1, D_HGRN), norm_g.reshape(1, D_HGRN).astype(F32),
      jnp.asarray(mat, BF16), jnp.asarray(lvl))


def _ret_kernel(q_ref, k_ref, v_ref, g_ref, cos_ref, sin_ref, gg_ref, gb_ref, o_ref,
                intra_ref, qdec_ref, kdec_ref, st_ref, *, tt, c):
    log_gamma = [math.log1p(-(2.0 ** (-5.0 - h))) for h in range(RET_HEADS)]

    @pl.when(pl.program_id(1) == 0)
    def _():
        st_ref[...] = jnp.zeros(st_ref.shape, F32)
        row = lax.broadcasted_iota(jnp.int32, (c, c), 0)
        col = lax.broadcasted_iota(jnp.int32, (c, c), 1)
        diff = (row - col).astype(F32)
        rowl = lax.broadcasted_iota(jnp.int32, (c, HEAD_DIM), 0).astype(F32)
        for h in range(RET_HEADS):
            intra_ref[h] = jnp.where(diff >= 0, jnp.exp(log_gamma[h] * jnp.maximum(diff, 0.0)), 0.0)
            qdec_ref[h] = jnp.exp(log_gamma[h] * (rowl + 1.0))
            kdec_ref[h] = jnp.exp(log_gamma[h] * (c - 1.0 - rowl))

    k_scale = HEAD_DIM ** -0.5

    unroll = _chunks_per_trip(tt, c, RET_UNROLL)
    tasks = [(u, h) for u in range(unroll) for h in range(RET_HEADS)]
    cols = [slice(h * HEAD_DIM, (h + 1) * HEAD_DIM) for h in range(RET_HEADS)]

    def chunk(ci, carry):
        base = ci * (unroll * c)
        rows = [pl.ds(pl.multiple_of(base + u * c, c), c) for u in range(unroll)]
        cs = [cos_ref[rows[u], :] for u in range(unroll)]
        sn = [sin_ref[rows[u], :] for u in range(unroll)]
        qr, kr, vh, scores, outs = {}, {}, {}, {}, {}
        for t in tasks:
            u, h = t
            qh = q_ref[rows[u], cols[h]]
            kh = k_ref[rows[u], cols[h]]
            qr[t] = qh * cs[u] + pltpu.roll(qh, HEAD_DIM // 2, 1) * sn[u]
            kr[t] = (kh * cs[u] + pltpu.roll(kh, HEAD_DIM // 2, 1) * sn[u]) * k_scale
            vh[t] = v_ref[rows[u], cols[h]].astype(BF16)
            scores[t] = _dot_nt(qr[t].astype(BF16), kr[t].astype(BF16))
        for t in tasks:
            u, h = t
            st = st_ref[h]
            masked = (scores[t] * intra_ref[h]).astype(BF16)
            outs[t] = _dot(masked, vh[t]) + _dot((qr[t] * qdec_ref[h]).astype(BF16), st.astype(BF16))
            st_ref[h] = math.exp(log_gamma[h] * c) * st + _dot_tn((kr[t] * kdec_ref[h]).astype(BF16), vh[t])
        for t in tasks:
            u, h = t
            o = outs[t]
            mu = jnp.mean(o, axis=-1, keepdims=True)
            oc = o - mu
            var = jnp.mean(oc * oc, axis=-1, keepdims=True)
            on = oc * lax.rsqrt(var + LN_EPS) * gg_ref[:, cols[h]] + gb_ref[:, cols[h]]
            gh = g_ref[rows[u], cols[h]]
            o_ref[rows[u], cols[h]] = (gh * _sigmoid(gh) * on).astype(o_ref.dtype)
        return carry

    lax.fori_loop(0, tt // (unroll * c), chunk, 0)


def _ret_mixer(proj, cos_t, sin_t, gn_g, gn_b, tt):
    b, t, _ = proj.shape
    c = min(RET_CHUNK, tt)
    blk = lambda n: pl.BlockSpec((None, tt, D_RET), lambda i, j: (i, j, RET_COL_BLOCK + n))
    tab = pl.BlockSpec((None, tt, HEAD_DIM), lambda i, j: (i, j, 0))
    vec = pl.BlockSpec((1, D_RET), lambda i, j: (0, 0))
    return pl.pallas_call(
        functools.partial(_ret_kernel, tt=tt, c=c),
        name="ret_mixer",
        grid=(b, t // tt),
        in_specs=[blk(0), blk(1), blk(2), blk(3), tab, tab, vec, vec],
        out_specs=pl.BlockSpec((None, tt, D_RET), lambda i, j: (i, j, 0)),
        out_shape=jax.ShapeDtypeStruct((b, t, D_RET), BF16),
        scratch_shapes=[pltpu.VMEM((RET_HEADS, c, c), F32),
                        pltpu.VMEM((RET_HEADS, c, HEAD_DIM), F32),
                        pltpu.VMEM((RET_HEADS, c, HEAD_DIM), F32),
                        pltpu.VMEM((RET_HEADS, HEAD_DIM, HEAD_DIM), F32)],
        compiler_params=_params("parallel", "arbitrary"),
    )(proj, proj, proj, proj, cos_t, sin_t, gn_g.reshape(1, D_RET).astype(F32), gn_b.reshape(1, D_RET).astype(F32))


def _outproj_kernel(yc_ref, yh_ref, yr_ref, w_ref, x_ref, g_ref, b_ref, o_ref, mix_ref, *, alpha):
    mix_ref[:, 0:D_CONV] = yc_ref[...]
    mix_ref[:, D_CONV:D_CONV + D_HGRN] = yh_ref[...]
    mix_ref[:, D_CONV + D_HGRN:] = yr_ref[...]
    tm = o_ref.shape[0]
    rows = min(OUT_ROWS, tm)
    for r0 in range(0, tm, rows):
        z = alpha * x_ref[r0:r0 + rows, :] + _dot(mix_ref[r0:r0 + rows, :], w_ref[...])
        o_ref[r0:r0 + rows, :] = _layer_norm_rows(z, g_ref[...], b_ref[...])


def _outproj(yc, yh, yr, w_stack, layer, x2d, ln_g, ln_b, alpha, tm):
    m, d = x2d.shape
    d_mix = D_CONV + D_HGRN + D_RET
    vec = pl.BlockSpec((1, d), lambda i: (0, 0))
    return pl.pallas_call(
        functools.partial(_outproj_kernel, alpha=alpha),
        name="out_proj_ln",
        grid=(m // tm,),
        in_specs=[pl.BlockSpec((tm, D_CONV), lambda i: (i, 0)),
                  pl.BlockSpec((tm, D_HGRN), lambda i: (i, 0)),
                  pl.BlockSpec((tm, D_RET), lambda i: (i, 0)),
                  pl.BlockSpec((None, d_mix, d), lambda i: (layer, 0, 0)),
                  pl.BlockSpec((tm, d), lambda i: (i, 0)),
                  vec, vec],
        out_specs=pl.BlockSpec((tm, d), lambda i: (i, 0)),
        out_shape=jax.ShapeDtypeStruct((m, d), F32),
        scratch_shapes=[pltpu.VMEM((tm, d_mix), BF16)],
        compiler_params=_params("parallel"),
    )(yc, yh, yr, w_stack, x2d, ln_g.reshape(1, d).astype(F32), ln_b.reshape(1, d).astype(F32))


def _ffn_kernel(x_ref, w1_ref, w2_ref, g_ref, b_ref, o_ref, xb_ref, *, alpha):
    f = pl.program_id(1)

    @pl.when(f == 0)
    def _():
        xb_ref[...] = x_ref[...].astype(BF16)
        o_ref[...] = jnp.zeros(o_ref.shape, F32)

    last = pl.num_programs(1) - 1

    def hidden():
        hid = jnp.maximum(_dot(xb_ref[...], w1_ref[...].astype(BF16)), 0.0)
        return (hid * hid).astype(BF16)

    @pl.when(f < last)
    def _():
        o_ref[...] += _dot(hidden(), w2_ref[...].astype(BF16))

    @pl.when(f == last)
    def _():
        hid = hidden()
        w2 = w2_ref[...].astype(BF16)
        tm = o_ref.shape[0]
        rows = min(OUT_ROWS, tm)
        for r0 in range(0, tm, rows):
            z = alpha * x_ref[r0:r0 + rows, :] + (o_ref[r0:r0 + rows, :] + _dot(hid[r0:r0 + rows, :], w2))
            o_ref[r0:r0 + rows, :] = _layer_norm_rows(z, g_ref[...], b_ref[...])


def _ffn(x2d, w1_stack, w2_stack, layer, ln_g, ln_b, alpha, tm, tf):
    m, d = x2d.shape
    d_ff = w1_stack.shape[-1]
    vec = pl.BlockSpec((1, d), lambda i, f: (0, 0))
    return pl.pallas_call(
        functools.partial(_ffn_kernel, alpha=alpha),
        name="ffn_ln",
        grid=(m // tm, d_ff // tf),
        in_specs=[pl.BlockSpec((tm, d), lambda i, f: (i, 0)),
                  pl.BlockSpec((None, d, tf), lambda i, f: (layer, 0, f)),
                  pl.BlockSpec((None, tf, d), lambda i, f: (layer, f, 0)),
                  vec, vec],
        out_specs=pl.BlockSpec((tm, d), lambda i, f: (i, 0)),
        out_shape=jax.ShapeDtypeStruct((m, d), F32),
        scratch_shapes=[pltpu.VMEM((tm, d), BF16)],
        compiler_params=pltpu.CompilerParams(dimension_semantics=("parallel", "arbitrary"),
                                             vmem_limit_bytes=FFN_VMEM_LIMIT_BYTES),
    )(x2d, w1_stack, w2_stack, ln_g.reshape(1, d).astype(F32), ln_b.reshape(1, d).astype(F32))


def kernel(x, positions, w_in, w_dw, b_dw, conv_ln_g, conv_ln_b, hgrn_lb, hgrn_norm_g, ret_gn_g, ret_gn_b,
           w_out, ln1_g, ln1_b, w_ff1, w_ff2, ln2_g, ln2_b):
    b, t, d = x.shape
    depth = w_in.shape[0]
    alpha = (2.0 * depth) ** 0.25
    m = b * t
    tt = min(512, t)
    tm = min(1024, m)
    tm_out = min(512, m)
    tn = 2 * D_CONV
    tf = 512

    w_in_b = w_in.astype(BF16)
    w_out_b = w_out.astype(BF16)

    lower = _lower_bounds(hgrn_lb)
    cos_t, sin_t = _rope_tables(positions, tt)

    x2d = x.reshape(m, d).astype(F32)
    for l in range(depth):
        proj = _proj(x2d, w_in_b, l, tm, tn).reshape(b, t, -1)
        yh = _hgrn_mixer(proj, lower[l], hgrn_norm_g[l], tt)
        yr = _ret_mixer(proj, cos_t, sin_t, ret_gn_g[l], ret_gn_b[l], tt)
        yc = _conv_mixer(proj, w_dw[l], b_dw[l], conv_ln_g[l], conv_ln_b[l], tt)
        x2d = _outproj(yc.reshape(m, D_CONV), yh.reshape(m, D_HGRN), yr.reshape(m, D_RET),
                       w_out_b, l, x2d, ln1_g[l], ln1_b[l], alpha, tm_out)
        x2d = _ffn(x2d, w_ff1, w_ff2, l, ln2_g[l], ln2_b[l], alpha, tm, tf)
    return x2d.reshape(b, t, d).astype(x.dtype)
```

```python
import functools
import math

import numpy as np
import jax
import jax.numpy as jnp
from jax import lax
from jax.experimental import pallas as pl
from jax.experimental.pallas import tpu as pltpu

F32 = jnp.float32
BF16 = jnp.bfloat16

D_CONV = 512
D_HGRN = 768
D_RET = 768
CONV_WIDTH = 31
HEAD_DIM = 128
HGRN_HEADS = D_HGRN // HEAD_DIM
RET_HEADS = D_RET // HEAD_DIM
ROPE_BASE = 10000.0
LN_EPS = 1e-5

LANES = 128
SUBLANES = 8
HGRN_CHUNK = 64
HGRN_LEVELS = 6
HGRN_PAIRS = HGRN_HEADS // 2
HGRN_UNROLL = 8
PAIR = 2 * HEAD_DIM
RET_CHUNK = 128
RET_UNROLL = 1
CONV_HALO = 32
CONV_ROWS = 128
NORM_ROWS = 64
OUT_ROWS = 256

VMEM_LIMIT_BYTES = 56 * 1024 * 1024
FFN_VMEM_LIMIT_BYTES = 62 * 1024 * 1024

RET_COL_BLOCK = 0
HGRN_COL_BLOCK = 4 * D_RET // D_HGRN
CONV_COL_BLOCK = (4 * D_HGRN + 4 * D_RET) // D_CONV
PROJ_F32_TILES = 2
HGRN_F_BLOCK_F32 = 1


def _dot(a, b):
    return jnp.dot(a, b, preferred_element_type=F32)


def _dot_nt(a, b):
    return lax.dot_general(a, b, (((1,), (1,)), ((), ())), preferred_element_type=F32)


def _dot_tn(a, b):
    return lax.dot_general(a, b, (((0,), (0,)), ((), ())), preferred_element_type=F32)


def _sigmoid(x):
    return 0.5 * jnp.tanh(0.5 * x) + 0.5


def _chunks_per_trip(tt, chunk, want):
    return max(u for u in range(1, want + 1) if tt % (u * chunk) == 0)


def _params(*semantics):
    return pltpu.CompilerParams(dimension_semantics=semantics, vmem_limit_bytes=VMEM_LIMIT_BYTES)


def _layer_norm_rows(z, g, b):
    mu = jnp.mean(z, axis=-1, keepdims=True)
    zc = z - mu
    var = jnp.mean(zc * zc, axis=-1, keepdims=True)
    return zc * lax.rsqrt(var + LN_EPS) * g + b


def _lower_bounds_kernel(lb_ref, o_ref):
    depth = lb_ref.shape[0]
    rows = [lb_ref[l:l + 1, :] for l in range(depth)]
    m = functools.reduce(jnp.maximum, rows)
    ex = [jnp.exp(r - m) for r in rows]
    tot = functools.reduce(jnp.add, ex)
    p = [e / tot for e in ex]
    c = p[0]
    for l in range(depth):
        if l > 0:
            c = c + p[l]
        o_ref[l:l + 1, :] = jnp.clip(c - p[0], 0.0, 1.0 - 1e-6)


def _lower_bounds(hgrn_lb):
    return pl.pallas_call(
        _lower_bounds_kernel,
        name="lower_bounds",
        out_shape=jax.ShapeDtypeStruct(hgrn_lb.shape, F32),
    )(hgrn_lb.astype(F32))


def _rope_kernel(pos_ref, inv_ref, cos_ref, sin_ref):
    ang = pos_ref[...].astype(F32) * inv_ref[...]
    lane = lax.broadcasted_iota(jnp.int32, ang.shape, 1)
    s = jnp.sin(ang)
    cos_ref[...] = jnp.cos(ang)
    sin_ref[...] = jnp.where(lane < HEAD_DIM // 2, -s, s)


def _rope_tables(positions, tt):
    b, t = positions.shape
    half = HEAD_DIM // 2
    inv = ROPE_BASE ** (-jnp.arange(half, dtype=F32) / half)
    inv2 = jnp.concatenate([inv, inv]).reshape(1, HEAD_DIM)
    tab = jax.ShapeDtypeStruct((b, t, HEAD_DIM), F32)
    return pl.pallas_call(
        _rope_kernel,
        name="rope_tables",
        grid=(b, t // tt),
        in_specs=[pl.BlockSpec((None, tt, 1), lambda i, j: (i, j, 0)),
                  pl.BlockSpec((1, HEAD_DIM), lambda i, j: (0, 0))],
        out_specs=[pl.BlockSpec((None, tt, HEAD_DIM), lambda i, j: (i, j, 0))] * 2,
        out_shape=[tab, tab],
        compiler_params=_params("parallel", "parallel"),
    )(positions.reshape(b, t, 1), inv2)


def _proj_kernel(x_ref, w_ref, o_ref, of_ref, xb_ref):
    @pl.when(pl.program_id(1) == 0)
    def _():
        xb_ref[...] = x_ref[...].astype(BF16)

    acc = _dot(xb_ref[...], w_ref[...])
    o_ref[...] = acc.astype(BF16)
    of_ref[...] = acc


def _proj(x2d, w_stack, layer, tm, tn):
    m, k = x2d.shape
    n = w_stack.shape[-1]
    n_blocks = n // tn
    conv_t, hgrn_t, ret_t = 2 * D_CONV // tn, 4 * D_HGRN // tn, 4 * D_RET // tn
    assert conv_t * tn == 2 * D_CONV and hgrn_t * tn == 4 * D_HGRN and conv_t + hgrn_t + ret_t == n_blocks
    n_f32 = PROJ_F32_TILES

    def out_tile(j):
        late = n_blocks - n_f32
        return jnp.where(j < ret_t, j, jnp.where(j < late, j + n_f32, j - (late - ret_t)))

    def src_tile(t):
        return jnp.where(t < ret_t, t + conv_t + hgrn_t, jnp.where(t < ret_t + hgrn_t, t - ret_t + conv_t, 0))

    return pl.pallas_call(
        _proj_kernel,
        name="in_proj",
        grid=(m // tm, n_blocks),
        in_specs=[pl.BlockSpec((tm, k), lambda i, j: (i, 0)),
                  pl.BlockSpec((None, k, tn), lambda i, j: (layer, 0, src_tile(out_tile(j))))],
        out_specs=[pl.BlockSpec((tm, tn), lambda i, j: (i, out_tile(j))),
                   pl.BlockSpec((tm, tn), lambda i, j: (i, jnp.maximum(j - (n_blocks - n_f32), 0)))],
        out_shape=[jax.ShapeDtypeStruct((m, n), BF16),
                   jax.ShapeDtypeStruct((m, n_f32 * tn), F32)],
        scratch_shapes=[pltpu.VMEM((tm, k), BF16)],
        compiler_params=_params("parallel", "arbitrary"),
    )(x2d, w_stack)


def _conv_kernel(a_ref, g_ref, w_ref, b_ref, lng_ref, lnb_ref, o_ref, u_ref, y_ref, *, tt):
    @pl.when(pl.program_id(1) == 0)
    def _():
        u_ref[0:CONV_HALO, :] = jnp.zeros((CONV_HALO, D_CONV), F32)

    @pl.when(pl.program_id(1) > 0)
    def _():
        u_ref[0:CONV_HALO, :] = u_ref[tt:tt + CONV_HALO, :]

    u_ref[CONV_HALO:CONV_HALO + tt, :] = a_ref[...].astype(F32) * _sigmoid(g_ref[...].astype(F32))

    first_tap = CONV_HALO - (CONV_WIDTH - 1)
    rows = min(CONV_ROWS, tt)
    win = rows + CONV_HALO
    for r0 in range(0, tt, rows):
        for c0 in range(0, D_CONV, LANES):
            acc = jnp.broadcast_to(b_ref[:, c0:c0 + LANES], (rows, LANES))
            for r in range(SUBLANES):
                window = u_ref[r0:r0 + win, c0:c0 + LANES]
                shifted = window if r == 0 else pltpu.roll(window, win - r, 0)
                for a in range(win // SUBLANES):
                    j = SUBLANES * a + r - first_tap
                    if 0 <= j < CONV_WIDTH:
                        acc = acc + w_ref[j:j + 1, c0:c0 + LANES] * shifted[SUBLANES * a:SUBLANES * a + rows]
            y_ref[r0:r0 + rows, c0:c0 + LANES] = acc

    nrows = min(NORM_ROWS, tt)
    for r0 in range(0, tt, nrows):
        v = _layer_norm_rows(y_ref[r0:r0 + nrows, :], lng_ref[...], lnb_ref[...])
        o_ref[r0:r0 + nrows, :] = (v * _sigmoid(v)).astype(o_ref.dtype)


def _conv_mixer(proj, w_dw, b_dw, ln_g, ln_b, tt):
    b, t, _ = proj.shape
    row = lambda v: v.reshape(1, D_CONV).astype(F32)
    vec = pl.BlockSpec((1, D_CONV), lambda i, j: (0, 0))
    blk = lambda n: pl.BlockSpec((None, tt, D_CONV), lambda i, j: (i, j, CONV_COL_BLOCK + n))
    return pl.pallas_call(
        functools.partial(_conv_kernel, tt=tt),
        name="conv_mixer",
        grid=(b, t // tt),
        in_specs=[blk(0), blk(1),
                  pl.BlockSpec((CONV_WIDTH, D_CONV), lambda i, j: (0, 0)),
                  vec, vec, vec],
        out_specs=pl.BlockSpec((None, tt, D_CONV), lambda i, j: (i, j, 0)),
        out_shape=jax.ShapeDtypeStruct((b, t, D_CONV), BF16),
        scratch_shapes=[pltpu.VMEM((CONV_HALO + tt, D_CONV), F32),
                        pltpu.VMEM((tt, D_CONV), F32)],
        compiler_params=_params("parallel", "arbitrary"),
    )(proj, proj, w_dw.astype(F32), row(b_dw), row(ln_g), row(ln_b))


def _hgrn_tables():
    c = HGRN_CHUNK
    mat = np.zeros((HGRN_LEVELS, c, c), np.float32)
    for l in range(1, HGRN_LEVELS):
        half = 1 << l
        for t in range(c):
            mid = ((t >> (l + 1)) << (l + 1)) + half - 1
            if t > mid:
                mat[l - 1, t, mid + 1:t + 1] = 1.0
            else:
                mat[l - 1, t, t + 1:mid + 1] = 1.0
    mat[HGRN_LEVELS - 1] = np.tril(np.ones((c, c), np.float32))
    mat = mat.reshape(-1, c)
    t = np.arange(c)[:, None]
    s = np.arange(c)[None, :]
    x = t ^ s
    lvl = np.zeros((c, c), np.int32)
    for l in range(1, HGRN_LEVELS):
        lvl += (x >= (1 << l)).astype(np.int32)
    lvl = np.where(t == s, HGRN_LEVELS, lvl)
    lvl = np.where(t < s, HGRN_LEVELS + 1, lvl)
    return np.concatenate([mat, mat, mat], axis=1), np.concatenate([lvl, lvl], axis=1).astype(np.int32)


def _hgrn_kernel(q_ref, f_ref, i_ref, g_ref, lb_ref, ng_ref, mat_ref, lvl_ref, o_ref,
                 qs_ref, ks_ref, lf_ref, e_ref, kb_ref, ib_ref, st_ref, *, tt):
    c = HGRN_CHUNK

    @pl.when(pl.program_id(1) == 0)
    def _():
        st_ref[...] = jnp.zeros(st_ref.shape, F32)
        kb_ref[...] = jnp.zeros(kb_ref.shape, BF16)
        ib_ref[...] = jnp.zeros(ib_ref.shape, BF16)

    q = q_ref[...].astype(F32)
    qs_ref[...] = q * _sigmoid(q)
    z = f_ref[...]
    ez = jnp.exp(-jnp.abs(z))
    inv = 1.0 / (1.0 + ez)
    log_sig = jnp.minimum(z, 0.0) - jnp.log(1.0 + ez)
    lb = lb_ref[...]
    f_pos = lb + (1.0 - lb) * (jnp.where(z >= 0, 1.0, ez) * inv)
    lf_ref[...] = jnp.where(lb > 0, jnp.log(f_pos), log_sig)
    ks_ref[...] = (1.0 - lb) * (jnp.where(z >= 0, ez, 1.0) * inv)

    lvl = lvl_ref[...]
    odd_row = (lax.broadcasted_iota(jnp.int32, (c, PAIR), 0) & 1) == 1

    unroll = _chunks_per_trip(tt, c, HGRN_UNROLL)
    tasks = [(u, p) for u in range(unroll) for p in range(HGRN_PAIRS)]

    def chunk(ci, carry):
        base = ci * (unroll * c)
        rows = [pl.ds(pl.multiple_of(base + u * c, c), c) for u in range(unroll)]
        lanes = [slice(p * PAIR, (p + 1) * PAIR) for p in range(HGRN_PAIRS)]
        slot = {t: n for n, t in enumerate(tasks)}

        lf, sums = {}, {}
        for t in tasks:
            u, p = t
            lf[t] = lf_ref[rows[u], lanes[p]]
            hi = lf[t].astype(BF16)
            r1 = lf[t] - hi.astype(F32)
            mid = r1.astype(BF16)
            lo = (r1 - mid.astype(F32)).astype(BF16)
            sums[t] = _dot(mat_ref[...], jnp.concatenate([hi, mid, lo], axis=0))

        q2, k2, i2, erem = {}, {}, {}, {}
        for t in tasks:
            u, p = t
            e_ref[slot[t]] = jnp.exp(sums[t])
            cum = sums[t][(HGRN_LEVELS - 1) * c:HGRN_LEVELS * c]
            erem[t] = jnp.exp(cum[c - 1:c, :] - cum)
            q2[t] = qs_ref[rows[u], lanes[p]]
            k2[t] = ks_ref[rows[u], lanes[p]]
            i2[t] = i_ref[rows[u], lanes[p]]
            ib_ref[slot[t], 0:c, 0:HEAD_DIM] = i2[t][:, 0:HEAD_DIM]
            ib_ref[slot[t], c:2 * c, HEAD_DIM:PAIR] = i2[t][:, HEAD_DIM:PAIR]

        def scores(t, level, el):
            kt = (k2[t] * el).astype(BF16) if el is not None else k2[t].astype(BF16)
            qt = (q2[t] * el).astype(BF16) if el is not None else q2[t].astype(BF16)
            kb_ref[slot[t], level, 0:c, 0:HEAD_DIM] = kt[:, 0:HEAD_DIM]
            kb_ref[slot[t], level, c:2 * c, HEAD_DIM:PAIR] = kt[:, HEAD_DIM:PAIR]
            return _dot_nt(qt, kb_ref[slot[t], level])

        att = {t: jnp.where(lvl == HGRN_LEVELS, scores(t, HGRN_LEVELS, None), 0.0) for t in tasks}
        for t in tasks:
            att[t] = jnp.where(lvl == 0, scores(t, 0, jnp.exp(jnp.where(odd_row, lf[t], 0.0))), att[t])
        for l in range(1, HGRN_LEVELS):
            for t in tasks:
                att[t] = jnp.where(lvl == l, scores(t, l, e_ref[slot[t], (l - 1) * c:l * c, :]), att[t])

        o2 = {t: _dot(att[t].astype(BF16), ib_ref[slot[t]]) for t in tasks}

        for t in tasks:
            u, p = t
            ecum = e_ref[slot[t], (HGRN_LEVELS - 1) * c:HGRN_LEVELS * c, :]
            glast = e_ref[slot[t], HGRN_LEVELS * c - 1:HGRN_LEVELS * c, :]
            qc = (q2[t] * ecum).astype(BF16)
            kr = (k2[t] * erem[t]).astype(BF16)
            gate = ng_ref[:, lanes[p]] * _sigmoid(g_ref[rows[u], lanes[p]].astype(F32))
            for hh in range(2):
                h = 2 * p + hh
                hl = slice(hh * HEAD_DIM, (hh + 1) * HEAD_DIM)
                st = st_ref[h]
                o = o2[t][:, hl] + _dot_nt(qc[:, hl], st.astype(BF16))
                st_ref[h] = st * glast[:, hl] + _dot_tn(i2[t][:, hl], kr[:, hl])
                o = o * lax.rsqrt(jnp.mean(o * o, axis=-1, keepdims=True) + LN_EPS)
                o_ref[rows[u], h * HEAD_DIM:(h + 1) * HEAD_DIM] = (o * gate[:, hl]).astype(o_ref.dtype)
        return carry

    lax.fori_loop(0, tt // (unroll * c), chunk, 0)


def _hgrn_mixer(proj, proj_f32, lower, norm_g, tt):
    b, t, _ = proj.shape
    mat, lvl = _hgrn_tables()
    c = HGRN_CHUNK
    blk = lambda n: pl.BlockSpec((None, tt, D_HGRN), lambda i, j: (i, j, HGRN_COL_BLOCK + n))
    gate_blk = pl.BlockSpec((None, tt, D_HGRN), lambda i, j: (i, j, HGRN_F_BLOCK_F32))
    vec = pl.BlockSpec((1, D_HGRN), lambda i, j: (0, 0))
    return pl.pallas_call(
        functools.partial(_hgrn_kernel, tt=tt),
        name="hgrn_mixer",
        grid=(b, t // tt),
        in_specs=[blk(0), gate_blk, blk(2), blk(3), vec, vec,
                  pl.BlockSpec(mat.shape, lambda i, j: (0, 0)),
                  pl.BlockSpec(lvl.shape, lambda i, j: (0, 0))],
        out_specs=pl.BlockSpec((None, tt, D_HGRN), lambda i, j: (i, j, 0)),
        out_shape=jax.ShapeDtypeStruct((b, t, D_HGRN), BF16),
        scratch_shapes=[pltpu.VMEM((tt, D_HGRN), F32),
                        pltpu.VMEM((tt, D_HGRN), F32),
                        pltpu.VMEM((tt, D_HGRN), F32),
                        pltpu.VMEM((HGRN_UNROLL * HGRN_PAIRS, HGRN_LEVELS * c, PAIR), F32),
                        pltpu.VMEM((HGRN_UNROLL * HGRN_PAIRS, HGRN_LEVELS + 1, 2 * c, PAIR), BF16),
                        pltpu.VMEM((HGRN_UNROLL * HGRN_PAIRS, 2 * c, PAIR), BF16),
                        pltpu.VMEM((HGRN_HEADS, HEAD_DIM, HEAD_DIM), F32)],
        compiler_params=_params("parallel", "arbitrary"),
    )(proj, proj_f32, proj, proj, lower.reshape(1, D_HGRN), norm_g.reshape(1, D_HGRN).astype(F32),
      jnp.asarray(mat, BF16), jnp.asarray(lvl))


def _ret_kernel(q_ref, k_ref, v_ref, g_ref, cos_ref, sin_ref, gg_ref, gb_ref, o_ref,
                intra_ref, qdec_ref, kdec_ref, st_ref, *, tt, c):
    log_gamma = [math.log1p(-(2.0 ** (-5.0 - h))) for h in range(RET_HEADS)]

    @pl.when(pl.program_id(1) == 0)
    def _():
        st_ref[...] = jnp.zeros(st_ref.shape, F32)
        row = lax.broadcasted_iota(jnp.int32, (c, c), 0)
        col = lax.broadcasted_iota(jnp.int32, (c, c), 1)
        diff = (row - col).astype(F32)
        rowl = lax.broadcasted_iota(jnp.int32, (c, HEAD_DIM), 0).astype(F32)
        for h in range(RET_HEADS):
            intra_ref[h] = jnp.where(diff >= 0, jnp.exp(log_gamma[h] * jnp.maximum(diff, 0.0)), 0.0)
            qdec_ref[h] = jnp.exp(log_gamma[h] * (rowl + 1.0))
            kdec_ref[h] = jnp.exp(log_gamma[h] * (c - 1.0 - rowl))

    k_scale = HEAD_DIM ** -0.5

    unroll = _chunks_per_trip(tt, c, RET_UNROLL)
    tasks = [(u, h) for u in range(unroll) for h in range(RET_HEADS)]
    cols = [slice(h * HEAD_DIM, (h + 1) * HEAD_DIM) for h in range(RET_HEADS)]

    def chunk(ci, carry):
        base = ci * (unroll * c)
        rows = [pl.ds(pl.multiple_of(base + u * c, c), c) for u in range(unroll)]
        cs = [cos_ref[rows[u], :] for u in range(unroll)]
        sn = [sin_ref[rows[u], :] for u in range(unroll)]
        qr, kr, vh, scores, outs = {}, {}, {}, {}, {}
        for t in tasks:
            u, h = t
            qh = q_ref[rows[u], cols[h]].astype(F32)
            kh = k_ref[rows[u], cols[h]].astype(F32)
            qr[t] = qh * cs[u] + pltpu.roll(qh, HEAD_DIM // 2, 1) * sn[u]
            kr[t] = (kh * cs[u] + pltpu.roll(kh, HEAD_DIM // 2, 1) * sn[u]) * k_scale
            vh[t] = v_ref[rows[u], cols[h]]
            scores[t] = _dot_nt(qr[t].astype(BF16), kr[t].astype(BF16))
        for t in tasks:
            u, h = t
            st = st_ref[h]
            masked = (scores[t] * intra_ref[h]).astype(BF16)
            outs[t] = _dot(masked, vh[t]) + _dot((qr[t] * qdec_ref[h]).astype(BF16), st.astype(BF16))
            st_ref[h] = math.exp(log_gamma[h] * c) * st + _dot_tn((kr[t] * kdec_ref[h]).astype(BF16), vh[t])
        for t in tasks:
            u, h = t
            o = outs[t]
            mu = jnp.mean(o, axis=-1, keepdims=True)
            oc = o - mu
            var = jnp.mean(oc * oc, axis=-1, keepdims=True)
            on = oc * lax.rsqrt(var + LN_EPS) * gg_ref[:, cols[h]] + gb_ref[:, cols[h]]
            gh = g_ref[rows[u], cols[h]].astype(F32)
            o_ref[rows[u], cols[h]] = (gh * _sigmoid(gh) * on).astype(o_ref.dtype)
        return carry

    lax.fori_loop(0, tt // (unroll * c), chunk, 0)


def _ret_mixer(proj, cos_t, sin_t, gn_g, gn_b, tt):
    b, t, _ = proj.shape
    c = min(RET_CHUNK, tt)
    blk = lambda n: pl.BlockSpec((None, tt, D_RET), lambda i, j: (i, j, RET_COL_BLOCK + n))
    tab = pl.BlockSpec((None, tt, HEAD_DIM), lambda i, j: (i, j, 0))
    vec = pl.BlockSpec((1, D_RET), lambda i, j: (0, 0))
    return pl.pallas_call(
        functools.partial(_ret_kernel, tt=tt, c=c),
        name="ret_mixer",
        grid=(b, t // tt),
        in_specs=[blk(0), blk(1), blk(2), blk(3), tab, tab, vec, vec],
        out_specs=pl.BlockSpec((None, tt, D_RET), lambda i, j: (i, j, 0)),
        out_shape=jax.ShapeDtypeStruct((b, t, D_RET), BF16),
        scratch_shapes=[pltpu.VMEM((RET_HEADS, c, c), F32),
                        pltpu.VMEM((RET_HEADS, c, HEAD_DIM), F32),
                        pltpu.VMEM((RET_HEADS, c, HEAD_DIM), F32),
                        pltpu.VMEM((RET_HEADS, HEAD_DIM, HEAD_DIM), F32)],
        compiler_params=_params("parallel", "arbitrary"),
    )(proj, proj, proj, proj, cos_t, sin_t, gn_g.reshape(1, D_RET).astype(F32), gn_b.reshape(1, D_RET).astype(F32))


def _outproj_kernel(yc_ref, yh_ref, yr_ref, w_ref, x_ref, g_ref, b_ref, o_ref, mix_ref, *, alpha):
    mix_ref[:, 0:D_CONV] = yc_ref[...]
    mix_ref[:, D_CONV:D_CONV + D_HGRN] = yh_ref[...]
    mix_ref[:, D_CONV + D_HGRN:] = yr_ref[...]
    tm = o_ref.shape[0]
    rows = min(OUT_ROWS, tm)
    for r0 in range(0, tm, rows):
        z = alpha * x_ref[r0:r0 + rows, :] + _dot(mix_ref[r0:r0 + rows, :], w_ref[...])
        o_ref[r0:r0 + rows, :] = _layer_norm_rows(z, g_ref[...], b_ref[...])


def _outproj(yc, yh, yr, w_stack, layer, x2d, ln_g, ln_b, alpha, tm):
    m, d = x2d.shape
    d_mix = D_CONV + D_HGRN + D_RET
    vec = pl.BlockSpec((1, d), lambda i: (0, 0))
    return pl.pallas_call(
        functools.partial(_outproj_kernel, alpha=alpha),
        name="out_proj_ln",
        grid=(m // tm,),
        in_specs=[pl.BlockSpec((tm, D_CONV), lambda i: (i, 0)),
                  pl.BlockSpec((tm, D_HGRN), lambda i: (i, 0)),
                  pl.BlockSpec((tm, D_RET), lambda i: (i, 0)),
                  pl.BlockSpec((None, d_mix, d), lambda i: (layer, 0, 0)),
                  pl.BlockSpec((tm, d), lambda i: (i, 0)),
                  vec, vec],
        out_specs=pl.BlockSpec((tm, d), lambda i: (i, 0)),
        out_shape=jax.ShapeDtypeStruct((m, d), F32),
        scratch_shapes=[pltpu.VMEM((tm, d_mix), BF16)],
        compiler_params=_params("parallel"),
    )(yc, yh, yr, w_stack, x2d, ln_g.reshape(1, d).astype(F32), ln_b.reshape(1, d).astype(F32))


def _ffn_kernel(x_ref, w1_ref, w2_ref, g_ref, b_ref, o_ref, xb_ref, *, alpha):
    f = pl.program_id(1)

    @pl.when(f == 0)
    def _():
        xb_ref[...] = x_ref[...].astype(BF16)
        o_ref[...] = jnp.zeros(o_ref.shape, F32)

    last = pl.num_programs(1) - 1

    def hidden():
        hid = jnp.maximum(_dot(xb_ref[...], w1_ref[...].astype(BF16)), 0.0)
        return (hid * hid).astype(BF16)

    @pl.when(f < last)
    def _():
        o_ref[...] += _dot(hidden(), w2_ref[...].astype(BF16))

    @pl.when(f == last)
    def _():
        hid = hidden()
        w2 = w2_ref[...].astype(BF16)
        tm = o_ref.shape[0]
        rows = min(OUT_ROWS, tm)
        for r0 in range(0, tm, rows):
            z = alpha * x_ref[r0:r0 + rows, :] + (o_ref[r0:r0 + rows, :] + _dot(hid[r0:r0 + rows, :], w2))
            o_ref[r0:r0 + rows, :] = _layer_norm_rows(z, g_ref[...], b_ref[...])


def _ffn(x2d, w1_stack, w2_stack, layer, ln_g, ln_b, alpha, tm, tf):
    m, d = x2d.shape
    d_ff = w1_stack.shape[-1]
    vec = pl.BlockSpec((1, d), lambda i, f: (0, 0))
    return pl.pallas_call(
        functools.partial(_ffn_kernel, alpha=alpha),
        name="ffn_ln",
        grid=(m // tm, d_ff // tf),
        in_specs=[pl.BlockSpec((tm, d), lambda i, f: (i, 0)),
                  pl.BlockSpec((None, d, tf), lambda i, f: (layer, 0, f)),
                  pl.BlockSpec((None, tf, d), lambda i, f: (layer, f, 0)),
                  vec, vec],
        out_specs=pl.BlockSpec((tm, d), lambda i, f: (i, 0)),
        out_shape=jax.ShapeDtypeStruct((m, d), F32),
        scratch_shapes=[pltpu.VMEM((tm, d), BF16)],
        compiler_params=pltpu.CompilerParams(dimension_semantics=("parallel", "arbitrary"),
                                             vmem_limit_bytes=FFN_VMEM_LIMIT_BYTES),
    )(x2d, w1_stack, w2_stack, ln_g.reshape(1, d).astype(F32), ln_b.reshape(1, d).astype(F32))


def kernel(x, positions, w_in, w_dw, b_dw, conv_ln_g, conv_ln_b, hgrn_lb, hgrn_norm_g, ret_gn_g, ret_gn_b,
           w_out, ln1_g, ln1_b, w_ff1, w_ff2, ln2_g, ln2_b):
    b, t, d = x.shape
    depth = w_in.shape[0]
    alpha = (2.0 * depth) ** 0.25
    m = b * t
    tt = min(512, t)
    tm = min(1024, m)
    tm_out = min(512, m)
    tn = 2 * D_CONV
    tf = 512

    w_in_b = w_in.astype(BF16)
    w_out_b = w_out.astype(BF16)

    lower = _lower_bounds(hgrn_lb)
    cos_t, sin_t = _rope_tables(positions, tt)

    x2d = x.reshape(m, d).astype(F32)
    for l in range(depth):
        proj, proj_f32 = _proj(x2d, w_in_b, l, tm, tn)
        proj, proj_f32 = proj.reshape(b, t, -1), proj_f32.reshape(b, t, -1)
        yh = _hgrn_mixer(proj, proj_f32, lower[l], hgrn_norm_g[l], tt)
        yr = _ret_mixer(proj, cos_t, sin_t, ret_gn_g[l], ret_gn_b[l], tt)
        yc = _conv_mixer(proj, w_dw[l], b_dw[l], conv_ln_g[l], conv_ln_b[l], tt)
        x2d = _outproj(yc.reshape(m, D_CONV), yh.reshape(m, D_HGRN), yr.reshape(m, D_RET),
                       w_out_b, l, x2d, ln1_g[l], ln1_b[l], alpha, tm_out)
        x2d = _ffn(x2d, w_ff1, w_ff2, l, ln2_g[l], ln2_b[l], alpha, tm, tf)
    return x2d.reshape(b, t, d).astype(x.dtype)
```

```python
import functools
import math

import numpy as np
import jax
import jax.numpy as jnp
from jax import lax
from jax.experimental import pallas as pl
from jax.experimental.pallas import tpu as pltpu

F32 = jnp.float32
BF16 = jnp.bfloat16

D_CONV = 512
D_HGRN = 768
D_RET = 768
CONV_WIDTH = 31
HEAD_DIM = 128
HGRN_HEADS = D_HGRN // HEAD_DIM
RET_HEADS = D_RET // HEAD_DIM
ROPE_BASE = 10000.0
LN_EPS = 1e-5

LANES = 128
SUBLANES = 8
HGRN_CHUNK = 64
HGRN_LEVELS = 6
HGRN_PAIRS = HGRN_HEADS // 2
HGRN_UNROLL = 8
PAIR = 2 * HEAD_DIM
RET_CHUNK = 128
RET_UNROLL = 1
CONV_HALO = 32
CONV_ROWS = 128
NORM_ROWS = 64
OUT_ROWS = 128

VMEM_LIMIT_BYTES = 56 * 1024 * 1024
FFN_VMEM_LIMIT_BYTES = 62 * 1024 * 1024

HGRN_COL_BLOCK = 0
RET_COL_BLOCK = 4 * D_HGRN // D_RET
CONV_COL_BLOCK = (4 * D_HGRN + 4 * D_RET) // D_CONV


def _dot(a, b):
    return jnp.dot(a, b, preferred_element_type=F32)


def _dot_nt(a, b):
    return lax.dot_general(a, b, (((1,), (1,)), ((), ())), preferred_element_type=F32)


def _dot_tn(a, b):
    return lax.dot_general(a, b, (((0,), (0,)), ((), ())), preferred_element_type=F32)


def _sigmoid(x):
    return 0.5 * jnp.tanh(0.5 * x) + 0.5


def _chunks_per_trip(tt, chunk, want):
    return max(u for u in range(1, want + 1) if tt % (u * chunk) == 0)


def _params(*semantics):
    return pltpu.CompilerParams(dimension_semantics=semantics, vmem_limit_bytes=VMEM_LIMIT_BYTES)


def _layer_norm_rows(z, g, b):
    mu = jnp.mean(z, axis=-1, keepdims=True)
    zc = z - mu
    var = jnp.mean(zc * zc, axis=-1, keepdims=True)
    return zc * lax.rsqrt(var + LN_EPS) * g + b


def _lower_bounds_kernel(lb_ref, o_ref):
    depth = lb_ref.shape[0]
    rows = [lb_ref[l:l + 1, :] for l in range(depth)]
    m = functools.reduce(jnp.maximum, rows)
    ex = [jnp.exp(r - m) for r in rows]
    tot = functools.reduce(jnp.add, ex)
    p = [e / tot for e in ex]
    c = p[0]
    for l in range(depth):
        if l > 0:
            c = c + p[l]
        o_ref[l:l + 1, :] = jnp.clip(c - p[0], 0.0, 1.0 - 1e-6)


def _lower_bounds(hgrn_lb):
    return pl.pallas_call(
        _lower_bounds_kernel,
        name="lower_bounds",
        out_shape=jax.ShapeDtypeStruct(hgrn_lb.shape, F32),
    )(hgrn_lb.astype(F32))


def _rope_kernel(pos_ref, inv_ref, cos_ref, sin_ref):
    ang = pos_ref[...].astype(F32) * inv_ref[...]
    lane = lax.broadcasted_iota(jnp.int32, ang.shape, 1)
    s = jnp.sin(ang)
    cos_ref[...] = jnp.cos(ang)
    sin_ref[...] = jnp.where(lane < HEAD_DIM // 2, -s, s)


def _rope_tables(positions, tt):
    b, t = positions.shape
    half = HEAD_DIM // 2
    inv = ROPE_BASE ** (-jnp.arange(half, dtype=F32) / half)
    inv2 = jnp.concatenate([inv, inv]).reshape(1, HEAD_DIM)
    tab = jax.ShapeDtypeStruct((b, t, HEAD_DIM), F32)
    return pl.pallas_call(
        _rope_kernel,
        name="rope_tables",
        grid=(b, t // tt),
        in_specs=[pl.BlockSpec((None, tt, 1), lambda i, j: (i, j, 0)),
                  pl.BlockSpec((1, HEAD_DIM), lambda i, j: (0, 0))],
        out_specs=[pl.BlockSpec((None, tt, HEAD_DIM), lambda i, j: (i, j, 0))] * 2,
        out_shape=[tab, tab],
        compiler_params=_params("parallel", "parallel"),
    )(positions.reshape(b, t, 1), inv2)


def _proj_kernel(x_ref, w_ref, o_ref, xb_ref):
    @pl.when(pl.program_id(1) == 0)
    def _():
        xb_ref[...] = x_ref[...].astype(BF16)

    o_ref[...] = _dot(xb_ref[...], w_ref[...])


def _proj(x2d, w_stack, layer, tm, tn):
    m, k = x2d.shape
    n = w_stack.shape[-1]
    n_blocks = n // tn
    first = 2 * D_CONV // tn
    return pl.pallas_call(
        _proj_kernel,
        name="in_proj",
        grid=(m // tm, n_blocks),
        in_specs=[pl.BlockSpec((tm, k), lambda i, j: (i, 0)),
                  pl.BlockSpec((None, k, tn), lambda i, j: (layer, 0, (j + first) % n_blocks))],
        out_specs=pl.BlockSpec((tm, tn), lambda i, j: (i, j)),
        out_shape=jax.ShapeDtypeStruct((m, n), F32),
        scratch_shapes=[pltpu.VMEM((tm, k), BF16)],
        compiler_params=_params("parallel", "arbitrary"),
    )(x2d, w_stack)


def _conv_kernel(a_ref, g_ref, w_ref, b_ref, lng_ref, lnb_ref, o_ref, u_ref, y_ref, *, tt):
    @pl.when(pl.program_id(1) == 0)
    def _():
        u_ref[0:CONV_HALO, :] = jnp.zeros((CONV_HALO, D_CONV), F32)

    @pl.when(pl.program_id(1) > 0)
    def _():
        u_ref[0:CONV_HALO, :] = u_ref[tt:tt + CONV_HALO, :]

    u_ref[CONV_HALO:CONV_HALO + tt, :] = a_ref[...] * _sigmoid(g_ref[...])

    first_tap = CONV_HALO - (CONV_WIDTH - 1)
    rows = min(CONV_ROWS, tt)
    win = rows + CONV_HALO
    for r0 in range(0, tt, rows):
        for c0 in range(0, D_CONV, LANES):
            acc = jnp.broadcast_to(b_ref[:, c0:c0 + LANES], (rows, LANES))
            for r in range(SUBLANES):
                window = u_ref[r0:r0 + win, c0:c0 + LANES]
                shifted = window if r == 0 else pltpu.roll(window, win - r, 0)
                for a in range(win // SUBLANES):
                    j = SUBLANES * a + r - first_tap
                    if 0 <= j < CONV_WIDTH:
                        acc = acc + w_ref[j:j + 1, c0:c0 + LANES] * shifted[SUBLANES * a:SUBLANES * a + rows]
            y_ref[r0:r0 + rows, c0:c0 + LANES] = acc

    nrows = min(NORM_ROWS, tt)
    for r0 in range(0, tt, nrows):
        v = _layer_norm_rows(y_ref[r0:r0 + nrows, :], lng_ref[...], lnb_ref[...])
        o_ref[r0:r0 + nrows, :] = (v * _sigmoid(v)).astype(o_ref.dtype)


def _conv_mixer(proj, w_dw, b_dw, ln_g, ln_b, tt):
    b, t, _ = proj.shape
    row = lambda v: v.reshape(1, D_CONV).astype(F32)
    vec = pl.BlockSpec((1, D_CONV), lambda i, j: (0, 0))
    blk = lambda n: pl.BlockSpec((None, tt, D_CONV), lambda i, j: (i, j, CONV_COL_BLOCK + n))
    return pl.pallas_call(
        functools.partial(_conv_kernel, tt=tt),
        name="conv_mixer",
        grid=(b, t // tt),
        in_specs=[blk(0), blk(1),
                  pl.BlockSpec((CONV_WIDTH, D_CONV), lambda i, j: (0, 0)),
                  vec, vec, vec],
        out_specs=pl.BlockSpec((None, tt, D_CONV), lambda i, j: (i, j, 0)),
        out_shape=jax.ShapeDtypeStruct((b, t, D_CONV), BF16),
        scratch_shapes=[pltpu.VMEM((CONV_HALO + tt, D_CONV), F32),
                        pltpu.VMEM((tt, D_CONV), F32)],
        compiler_params=_params("parallel", "arbitrary"),
    )(proj, proj, w_dw.astype(F32), row(b_dw), row(ln_g), row(ln_b))


def _hgrn_tables():
    c = HGRN_CHUNK
    mat = np.zeros((HGRN_LEVELS, c, c), np.float32)
    for l in range(1, HGRN_LEVELS):
        half = 1 << l
        for t in range(c):
            mid = ((t >> (l + 1)) << (l + 1)) + half - 1
            if t > mid:
                mat[l - 1, t, mid + 1:t + 1] = 1.0
            else:
                mat[l - 1, t, t + 1:mid + 1] = 1.0
    mat[HGRN_LEVELS - 1] = np.tril(np.ones((c, c), np.float32))
    mat = mat.reshape(-1, c)
    t = np.arange(c)[:, None]
    s = np.arange(c)[None, :]
    x = t ^ s
    lvl = np.zeros((c, c), np.int32)
    for l in range(1, HGRN_LEVELS):
        lvl += (x >= (1 << l)).astype(np.int32)
    lvl = np.where(t == s, HGRN_LEVELS, lvl)
    lvl = np.where(t < s, HGRN_LEVELS + 1, lvl)
    return np.concatenate([mat, mat, mat], axis=1), np.concatenate([lvl, lvl], axis=1).astype(np.int32)


def _hgrn_kernel(q_ref, f_ref, i_ref, g_ref, lb_ref, ng_ref, mat_ref, lvl_ref, o_ref,
                 qs_ref, ks_ref, lf_ref, e_ref, kb_ref, ib_ref, st_ref, *, tt):
    c = HGRN_CHUNK

    @pl.when(pl.program_id(1) == 0)
    def _():
        st_ref[...] = jnp.zeros(st_ref.shape, F32)
        kb_ref[...] = jnp.zeros(kb_ref.shape, BF16)
        ib_ref[...] = jnp.zeros(ib_ref.shape, BF16)

    q = q_ref[...]
    qs_ref[...] = q * _sigmoid(q)
    z = f_ref[...]
    ez = jnp.exp(-jnp.abs(z))
    inv = 1.0 / (1.0 + ez)
    log_sig = jnp.minimum(z, 0.0) - jnp.log(1.0 + ez)
    lb = lb_ref[...]
    f_pos = lb + (1.0 - lb) * (jnp.where(z >= 0, 1.0, ez) * inv)
    lf_ref[...] = jnp.where(lb > 0, jnp.log(f_pos), log_sig)
    ks_ref[...] = (1.0 - lb) * (jnp.where(z >= 0, ez, 1.0) * inv)

    lvl = lvl_ref[...]
    odd_row = (lax.broadcasted_iota(jnp.int32, (c, PAIR), 0) & 1) == 1

    unroll = _chunks_per_trip(tt, c, HGRN_UNROLL)
    tasks = [(u, p) for u in range(unroll) for p in range(HGRN_PAIRS)]

    def chunk(ci, carry):
        base = ci * (unroll * c)
        rows = [pl.ds(pl.multiple_of(base + u * c, c), c) for u in range(unroll)]
        lanes = [slice(p * PAIR, (p + 1) * PAIR) for p in range(HGRN_PAIRS)]
        slot = {t: n for n, t in enumerate(tasks)}

        lf, sums = {}, {}
        for t in tasks:
            u, p = t
            lf[t] = lf_ref[rows[u], lanes[p]]
            hi = lf[t].astype(BF16)
            r1 = lf[t] - hi.astype(F32)
            mid = r1.astype(BF16)
            lo = (r1 - mid.astype(F32)).astype(BF16)
            sums[t] = _dot(mat_ref[...], jnp.concatenate([hi, mid, lo], axis=0))

        q2, k2, i2, erem = {}, {}, {}, {}
        for t in tasks:
            u, p = t
            e_ref[slot[t]] = jnp.exp(sums[t])
            cum = sums[t][(HGRN_LEVELS - 1) * c:HGRN_LEVELS * c]
            erem[t] = jnp.exp(cum[c - 1:c, :] - cum)
            q2[t] = qs_ref[rows[u], lanes[p]]
            k2[t] = ks_ref[rows[u], lanes[p]]
            i2[t] = i_ref[rows[u], lanes[p]].astype(BF16)
            ib_ref[slot[t], 0:c, 0:HEAD_DIM] = i2[t][:, 0:HEAD_DIM]
            ib_ref[slot[t], c:2 * c, HEAD_DIM:PAIR] = i2[t][:, HEAD_DIM:PAIR]

        def scores(t, level, el):
            kt = (k2[t] * el).astype(BF16) if el is not None else k2[t].astype(BF16)
            qt = (q2[t] * el).astype(BF16) if el is not None else q2[t].astype(BF16)
            kb_ref[slot[t], level, 0:c, 0:HEAD_DIM] = kt[:, 0:HEAD_DIM]
            kb_ref[slot[t], level, c:2 * c, HEAD_DIM:PAIR] = kt[:, HEAD_DIM:PAIR]
            return _dot_nt(qt, kb_ref[slot[t], level])

        att = {t: jnp.where(lvl == HGRN_LEVELS, scores(t, HGRN_LEVELS, None), 0.0) for t in tasks}
        for t in tasks:
            att[t] = jnp.where(lvl == 0, scores(t, 0, jnp.exp(jnp.where(odd_row, lf[t], 0.0))), att[t])
        for l in range(1, HGRN_LEVELS):
            for t in tasks:
                att[t] = jnp.where(lvl == l, scores(t, l, e_ref[slot[t], (l - 1) * c:l * c, :]), att[t])

        o2 = {t: _dot(att[t].astype(BF16), ib_ref[slot[t]]) for t in tasks}

        for t in tasks:
            u, p = t
            ecum = e_ref[slot[t], (HGRN_LEVELS - 1) * c:HGRN_LEVELS * c, :]
            glast = e_ref[slot[t], HGRN_LEVELS * c - 1:HGRN_LEVELS * c, :]
            qc = (q2[t] * ecum).astype(BF16)
            kr = (k2[t] * erem[t]).astype(BF16)
            gate = ng_ref[:, lanes[p]] * _sigmoid(g_ref[rows[u], lanes[p]])
            for hh in range(2):
                h = 2 * p + hh
                hl = slice(hh * HEAD_DIM, (hh + 1) * HEAD_DIM)
                st = st_ref[h]
                o = o2[t][:, hl] + _dot_nt(qc[:, hl], st.astype(BF16))
                st_ref[h] = st * glast[:, hl] + _dot_tn(i2[t][:, hl], kr[:, hl])
                o = o * lax.rsqrt(jnp.mean(o * o, axis=-1, keepdims=True) + LN_EPS)
                o_ref[rows[u], h * HEAD_DIM:(h + 1) * HEAD_DIM] = (o * gate[:, hl]).astype(o_ref.dtype)
        return carry

    lax.fori_loop(0, tt // (unroll * c), chunk, 0)


def _hgrn_mixer(proj, lower, norm_g, tt):
    b, t, _ = proj.shape
    mat, lvl = _hgrn_tables()
    c = HGRN_CHUNK
    blk = lambda n: pl.BlockSpec((None, tt, D_HGRN), lambda i, j: (i, j, HGRN_COL_BLOCK + n))
    vec = pl.BlockSpec((1, D_HGRN), lambda i, j: (0, 0))
    return pl.pallas_call(
        functools.partial(_hgrn_kernel, tt=tt),
        name="hgrn_mixer",
        grid=(b, t // tt),
        in_specs=[blk(0), blk(1), blk(2), blk(3), vec, vec,
                  pl.BlockSpec(mat.shape, lambda i, j: (0, 0)),
                  pl.BlockSpec(lvl.shape, lambda i, j: (0, 0))],
        out_specs=pl.BlockSpec((None, tt, D_HGRN), lambda i, j: (i, j, 0)),
        out_shape=jax.ShapeDtypeStruct((b, t, D_HGRN), BF16),
        scratch_shapes=[pltpu.VMEM((tt, D_HGRN), F32),
                        pltpu.VMEM((tt, D_HGRN), F32),
                        pltpu.VMEM((tt, D_HGRN), F32),
                        pltpu.VMEM((HGRN_UNROLL * HGRN_PAIRS, HGRN_LEVELS * c, PAIR), F32),
                        pltpu.VMEM((HGRN_UNROLL * HGRN_PAIRS, HGRN_LEVELS + 1, 2 * c, PAIR), BF16),
                        pltpu.VMEM((HGRN_UNROLL * HGRN_PAIRS, 2 * c, PAIR), BF16),
                        pltpu.VMEM((HGRN_HEADS, HEAD_DIM, HEAD_DIM), F32)],
        compiler_params=_params("parallel", "arbitrary"),
    )(proj, proj, proj, proj, lower.reshape(1, D_HGRN), norm_g.reshape(1, D_HGRN).astype(F32),
      jnp.asarray(mat, BF16), jnp.asarray(lvl))


def _ret_kernel(q_ref, k_ref, v_ref, g_ref, cos_ref, sin_ref, gg_ref, gb_ref, o_ref,
                intra_ref, qdec_ref, kdec_ref, st_ref, *, tt, c):
    log_gamma = [math.log1p(-(2.0 ** (-5.0 - h))) for h in range(RET_HEADS)]

    @pl.when(pl.program_id(1) == 0)
    def _():
        st_ref[...] = jnp.zeros(st_ref.shape, F32)
        row = lax.broadcasted_iota(jnp.int32, (c, c), 0)
        col = lax.broadcasted_iota(jnp.int32, (c, c), 1)
        diff = (row - col).astype(F32)
        rowl = lax.broadcasted_iota(jnp.int32, (c, HEAD_DIM), 0).astype(F32)
        for h in range(RET_HEADS):
            intra_ref[h] = jnp.where(diff >= 0, jnp.exp(log_gamma[h] * jnp.maximum(diff, 0.0)), 0.0)
            qdec_ref[h] = jnp.exp(log_gamma[h] * (rowl + 1.0))
            kdec_ref[h] = jnp.exp(log_gamma[h] * (c - 1.0 - rowl))

    k_scale = HEAD_DIM ** -0.5

    unroll = _chunks_per_trip(tt, c, RET_UNROLL)
    tasks = [(u, h) for u in range(unroll) for h in range(RET_HEADS)]
    cols = [slice(h * HEAD_DIM, (h + 1) * HEAD_DIM) for h in range(RET_HEADS)]

    def chunk(ci, carry):
        base = ci * (unroll * c)
        rows = [pl.ds(pl.multiple_of(base + u * c, c), c) for u in range(unroll)]
        cs = [cos_ref[rows[u], :] for u in range(unroll)]
        sn = [sin_ref[rows[u], :] for u in range(unroll)]
        qr, kr, vh, scores, outs = {}, {}, {}, {}, {}
        for t in tasks:
            u, h = t
            qh = q_ref[rows[u], cols[h]]
            kh = k_ref[rows[u], cols[h]]
            qr[t] = qh * cs[u] + pltpu.roll(qh, HEAD_DIM // 2, 1) * sn[u]
            kr[t] = (kh * cs[u] + pltpu.roll(kh, HEAD_DIM // 2, 1) * sn[u]) * k_scale
            vh[t] = v_ref[rows[u], cols[h]].astype(BF16)
            scores[t] = _dot_nt(qr[t].astype(BF16), kr[t].astype(BF16))
        for t in tasks:
            u, h = t
            st = st_ref[h]
            masked = (scores[t] * intra_ref[h]).astype(BF16)
            outs[t] = _dot(masked, vh[t]) + _dot((qr[t] * qdec_ref[h]).astype(BF16), st.astype(BF16))
            st_ref[h] = math.exp(log_gamma[h] * c) * st + _dot_tn((kr[t] * kdec_ref[h]).astype(BF16), vh[t])
        for t in tasks:
            u, h = t
            o = outs[t]
            mu = jnp.mean(o, axis=-1, keepdims=True)
            oc = o - mu
            var = jnp.mean(oc * oc, axis=-1, keepdims=True)
            on = oc * lax.rsqrt(var + LN_EPS) * gg_ref[:, cols[h]] + gb_ref[:, cols[h]]
            gh = g_ref[rows[u], cols[h]]
            o_ref[rows[u], cols[h]] = (gh * _sigmoid(gh) * on).astype(o_ref.dtype)
        return carry

    lax.fori_loop(0, tt // (unroll * c), chunk, 0)


def _ret_mixer(proj, cos_t, sin_t, gn_g, gn_b, tt):
    b, t, _ = proj.shape
    c = min(RET_CHUNK, tt)
    blk = lambda n: pl.BlockSpec((None, tt, D_RET), lambda i, j: (i, j, RET_COL_BLOCK + n))
    tab = pl.BlockSpec((None, tt, HEAD_DIM), lambda i, j: (i, j, 0))
    vec = pl.BlockSpec((1, D_RET), lambda i, j: (0, 0))
    return pl.pallas_call(
        functools.partial(_ret_kernel, tt=tt, c=c),
        name="ret_mixer",
        grid=(b, t // tt),
        in_specs=[blk(0), blk(1), blk(2), blk(3), tab, tab, vec, vec],
        out_specs=pl.BlockSpec((None, tt, D_RET), lambda i, j: (i, j, 0)),
        out_shape=jax.ShapeDtypeStruct((b, t, D_RET), BF16),
        scratch_shapes=[pltpu.VMEM((RET_HEADS, c, c), F32),
                        pltpu.VMEM((RET_HEADS, c, HEAD_DIM), F32),
                        pltpu.VMEM((RET_HEADS, c, HEAD_DIM), F32),
                        pltpu.VMEM((RET_HEADS, HEAD_DIM, HEAD_DIM), F32)],
        compiler_params=_params("parallel", "arbitrary"),
    )(proj, proj, proj, proj, cos_t, sin_t, gn_g.reshape(1, D_RET).astype(F32), gn_b.reshape(1, D_RET).astype(F32))


def _outproj_kernel(yc_ref, yh_ref, yr_ref, w_ref, x_ref, g_ref, b_ref, o_ref, mix_ref, *, alpha):
    mix_ref[:, 0:D_CONV] = yc_ref[...]
    mix_ref[:, D_CONV:D_CONV + D_HGRN] = yh_ref[...]
    mix_ref[:, D_CONV + D_HGRN:] = yr_ref[...]
    tm = o_ref.shape[0]
    rows = min(OUT_ROWS, tm)
    for r0 in range(0, tm, rows):
        z = alpha * x_ref[r0:r0 + rows, :] + _dot(mix_ref[r0:r0 + rows, :], w_ref[...])
        o_ref[r0:r0 + rows, :] = _layer_norm_rows(z, g_ref[...], b_ref[...])


def _outproj(yc, yh, yr, w_stack, layer, x2d, ln_g, ln_b, alpha, tm):
    m, d = x2d.shape
    d_mix = D_CONV + D_HGRN + D_RET
    vec = pl.BlockSpec((1, d), lambda i: (0, 0))
    return pl.pallas_call(
        functools.partial(_outproj_kernel, alpha=alpha),
        name="out_proj_ln",
        grid=(m // tm,),
        in_specs=[pl.BlockSpec((tm, D_CONV), lambda i: (i, 0)),
                  pl.BlockSpec((tm, D_HGRN), lambda i: (i, 0)),
                  pl.BlockSpec((tm, D_RET), lambda i: (i, 0)),
                  pl.BlockSpec((None, d_mix, d), lambda i: (layer, 0, 0)),
                  pl.BlockSpec((tm, d), lambda i: (i, 0)),
                  vec, vec],
        out_specs=pl.BlockSpec((tm, d), lambda i: (i, 0)),
        out_shape=jax.ShapeDtypeStruct((m, d), F32),
        scratch_shapes=[pltpu.VMEM((tm, d_mix), BF16)],
        compiler_params=_params("parallel"),
    )(yc, yh, yr, w_stack, x2d, ln_g.reshape(1, d).astype(F32), ln_b.reshape(1, d).astype(F32))


def _ffn_kernel(x_ref, w1_ref, w2_ref, g_ref, b_ref, o_ref, xb_ref, *, alpha):
    f = pl.program_id(1)

    @pl.when(f == 0)
    def _():
        xb_ref[...] = x_ref[...].astype(BF16)
        o_ref[...] = jnp.zeros(o_ref.shape, F32)

    last = pl.num_programs(1) - 1

    def hidden():
        hid = jnp.maximum(_dot(xb_ref[...], w1_ref[...].astype(BF16)), 0.0)
        return (hid * hid).astype(BF16)

    @pl.when(f < last)
    def _():
        o_ref[...] += _dot(hidden(), w2_ref[...].astype(BF16))

    @pl.when(f == last)
    def _():
        hid = hidden()
        w2 = w2_ref[...].astype(BF16)
        tm = o_ref.shape[0]
        rows = min(OUT_ROWS, tm)
        for r0 in range(0, tm, rows):
            z = alpha * x_ref[r0:r0 + rows, :] + (o_ref[r0:r0 + rows, :] + _dot(hid[r0:r0 + rows, :], w2))
            o_ref[r0:r0 + rows, :] = _layer_norm_rows(z, g_ref[...], b_ref[...])


def _ffn(x2d, w1_stack, w2_stack, layer, ln_g, ln_b, alpha, tm, tf):
    m, d = x2d.shape
    d_ff = w1_stack.shape[-1]
    vec = pl.BlockSpec((1, d), lambda i, f: (0, 0))
    return pl.pallas_call(
        functools.partial(_ffn_kernel, alpha=alpha),
        name="ffn_ln",
        grid=(m // tm, d_ff // tf),
        in_specs=[pl.BlockSpec((tm, d), lambda i, f: (i, 0)),
                  pl.BlockSpec((None, d, tf), lambda i, f: (layer, 0, f)),
                  pl.BlockSpec((None, tf, d), lambda i, f: (layer, f, 0)),
                  vec, vec],
        out_specs=pl.BlockSpec((tm, d), lambda i, f: (i, 0)),
        out_shape=jax.ShapeDtypeStruct((m, d), F32),
        scratch_shapes=[pltpu.VMEM((tm, d), BF16)],
        compiler_params=pltpu.CompilerParams(dimension_semantics=("parallel", "arbitrary"),
                                             vmem_limit_bytes=FFN_VMEM_LIMIT_BYTES),
    )(x2d, w1_stack, w2_stack, ln_g.reshape(1, d).astype(F32), ln_b.reshape(1, d).astype(F32))


def kernel(x, positions, w_in, w_dw, b_dw, conv_ln_g, conv_ln_b, hgrn_lb, hgrn_norm_g, ret_gn_g, ret_gn_b,
           w_out, ln1_g, ln1_b, w_ff1, w_ff2, ln2_g, ln2_b):
    b, t, d = x.shape
    depth = w_in.shape[0]
    alpha = (2.0 * depth) ** 0.25
    m = b * t
    tt = min(512, t)
    tm = min(1024, m)
    tm_out = min(512, m)
    tn = 2 * D_CONV
    tf = 512

    w_in_b = w_in.astype(BF16)
    w_out_b = w_out.astype(BF16)

    lower = _lower_bounds(hgrn_lb)
    cos_t, sin_t = _rope_tables(positions, tt)

    x2d = x.reshape(m, d).astype(F32)
    for l in range(depth):
        proj = _proj(x2d, w_in_b, l, tm, tn).reshape(b, t, -1)
        yh = _hgrn_mixer(proj, lower[l], hgrn_norm_g[l], tt)
        yr = _ret_mixer(proj, cos_t, sin_t, ret_gn_g[l], ret_gn_b[l], tt)
        yc = _conv_mixer(proj, w_dw[l], b_dw[l], conv_ln_g[l], conv_ln_b[l], tt)
        x2d = _outproj(yc.reshape(m, D_CONV), yh.reshape(m, D_HGRN), yr.reshape(m, D_RET),
                       w_out_b, l, x2d, ln1_g[l], ln1_b[l], alpha, tm_out)
        x2d = _ffn(x2d, w_ff1, w_ff2, l, ln2_g[l], ln2_b[l], alpha, tm, tf)
    return x2d.reshape(b, t, d).astype(x.dtype)
```
